```python
import jax, jax.numpy as jnp
from jax import lax
import numpy as np

D_MODEL = 1024
BATCH = 4
SEQ = 4096
DEPTH = 2

GROUP_HEADS = 4
HEAD_DIM = 64
GROUP_WIDTH = GROUP_HEADS * HEAD_DIM
N_GROUPS = 4
D_MIX = N_GROUPS * GROUP_WIDTH
BLOCK = 128
RET_CHUNK = 128
MLA_Q_RANK = 256
MLA_KV_RANK = 128
MLA_NOPE = 64
MLA_ROPE = 32
MLA_V = 64
ROPE_BASE = 10000.0
FORGET_BIAS_INIT = 4.0
PEER_HEADS = 8
PEER_KEYS = 128
PEER_EXPERTS = PEER_KEYS * PEER_KEYS
PEER_HALF = 128
PEER_QDIM = 2 * PEER_HALF
PEER_TOPK = 16
PEER_CHUNK = 128
EPS = 1e-6
IN_SPLITS = (GROUP_WIDTH,) * 4 + (GROUP_WIDTH,) * 3 + (GROUP_HEADS,) + (GROUP_WIDTH,) * 3 + (MLA_Q_RANK, MLA_KV_RANK, MLA_ROPE)
D_IN = sum(IN_SPLITS)

kernel_name = 'hybrid_ret_fox_sb_mla_peer'


def rms_norm(x, g):
    xf = x.astype(jnp.float32)
    y = xf * lax.rsqrt(jnp.mean(xf * xf, axis=-1, keepdims=True) + EPS)
    return (y * g.astype(jnp.float32)).astype(x.dtype)


def rope_tables(seq, dim):
    pos = jnp.arange(seq, dtype=jnp.float32)
    inv = ROPE_BASE ** (-jnp.arange(0, dim, 2, dtype=jnp.float32) / dim)
    ang = pos[:, None] * inv[None, :]
    return jnp.cos(ang), jnp.sin(ang)


def apply_rope(t, cos, sin):
    t1, t2 = jnp.split(t.astype(jnp.float32), 2, axis=-1)
    return jnp.concatenate([t1 * cos - t2 * sin, t1 * sin + t2 * cos], axis=-1)


def split_heads(t, n):
    b, s, _ = t.shape
    return t.reshape(b, s, n, -1).transpose(0, 2, 1, 3)


def head_norm(o, g, center):
    o = o.astype(jnp.float32)
    if center:
        o = o - jnp.mean(o, axis=-1, keepdims=True)
    o = o * lax.rsqrt(jnp.mean(o * o, axis=-1, keepdims=True) + EPS)
    b, h, s, d = o.shape
    return o.transpose(0, 2, 1, 3).reshape(b, s, h * d) * g.astype(jnp.float32)


def retention_chunkwise(q, k, v):
    b, h, s, d = q.shape
    dv = v.shape[-1]
    n = s // RET_CHUNK
    log_gamma = jnp.log1p(-(2.0 ** (-5.0 - jnp.arange(h, dtype=jnp.float32))))
    i = jnp.arange(RET_CHUNK, dtype=jnp.float32)
    rel = i[:, None] - i[None, :]
    intra = jnp.where(rel >= 0, jnp.exp(log_gamma[:, None, None] * jnp.maximum(rel, 0.0)), 0.0)
    q_decay = jnp.exp(log_gamma[:, None] * (i + 1.0))[:, :, None]
    k_decay = jnp.exp(log_gamma[:, None] * (RET_CHUNK - 1.0 - i))[:, :, None]
    chunk_decay = jnp.exp(log_gamma * RET_CHUNK)[:, None, None]

    def chunks(t):
        return t.astype(jnp.float32).reshape(b, h, n, RET_CHUNK, -1).transpose(2, 0, 1, 3, 4)

    def step(state, inp):
        qi, ki, vi = inp
        inner = jnp.einsum('bhid,bhjd->bhij', qi, ki) * intra
        out = jnp.einsum('bhij,bhjv->bhiv', inner, vi) + jnp.einsum('bhid,bhdv->bhiv', qi * q_decay, state)
        state = state * chunk_decay + jnp.einsum('bhjd,bhjv->bhdv', ki * k_decay, vi)
        return state, out

    state0 = jnp.zeros((b, h, d, dv), jnp.float32)
    _, out = lax.scan(step, state0, (chunks(q), chunks(k), chunks(v)))
    return out.transpose(1, 2, 0, 3, 4).reshape(b, h, s, dv)


def block_softmax_attention(q, k, v, scale, fcum=None):
    b, h, s, dk = q.shape
    n = s // BLOCK
    kf = k.astype(jnp.float32)
    vf = v.astype(jnp.float32)
    key_pos = jnp.arange(s)
    q_blocks = q.astype(jnp.float32).reshape(b, h, n, BLOCK, dk).transpose(2, 0, 1, 3, 4)
    xs = (jnp.arange(n), q_blocks)
    if fcum is not None:
        xs = xs + (fcum.reshape(b, h, n, BLOCK).transpose(2, 0, 1, 3),)

    def one_block(args):
        blk, qi = args[0], args[1]
        logits = jnp.einsum('bhqd,bhkd->bhqk', qi, kf) * scale
        if fcum is not None:
            logits = logits + args[2][..., :, None] - fcum[..., None, :]
        q_pos = blk * BLOCK + jnp.arange(BLOCK)
        logits = jnp.where(key_pos[None, :] <= q_pos[:, None], logits, -jnp.inf)
        p = jax.nn.softmax(logits, axis=-1)
        return jnp.einsum('bhqk,bhkv->bhqv', p, vf)

    out = lax.map(one_block, xs)
    return out.transpose(1, 2, 0, 3, 4).reshape(b, h, s, -1)


def stick_breaking_attention(q, k, v, scale):
    b, h, s, dk = q.shape
    n = s // BLOCK
    kf = k.astype(jnp.float32)
    vf = v.astype(jnp.float32)
    key_pos = jnp.arange(s)
    q_blocks = q.astype(jnp.float32).reshape(b, h, n, BLOCK, dk).transpose(2, 0, 1, 3, 4)

    def one_block(args):
        blk, qi = args
        z = jnp.einsum('bhqd,bhkd->bhqk', qi, kf) * scale
        q_pos = blk * BLOCK + jnp.arange(BLOCK)
        strict = key_pos[None, :] < q_pos[:, None]
        log_rest = jnp.where(strict, jax.nn.log_sigmoid(-z), 0.0)
        after = lax.cumsum(log_rest, axis=3, reverse=True) - log_rest
        w = jnp.where(strict, jnp.exp(jax.nn.log_sigmoid(z) + after), 0.0)
        return jnp.einsum('bhqk,bhkv->bhqv', w, vf)

    out = lax.map(one_block, (jnp.arange(n), q_blocks))
    return out.transpose(1, 2, 0, 3, 4).reshape(b, h, s, -1)


def latent_attention(cq_raw, ckv_raw, kr_raw, norm_cq, norm_ckv, w_uq, w_ukv, cos, sin):
    c_q = rms_norm(cq_raw, norm_cq)
    q = split_heads(c_q @ w_uq, GROUP_HEADS)
    q_nope, q_rope = q[..., :MLA_NOPE], q[..., MLA_NOPE:]
    c_kv = rms_norm(ckv_raw, norm_ckv)
    kv = split_heads(c_kv @ w_ukv, GROUP_HEADS)
    k_nope, v = kv[..., :MLA_NOPE], kv[..., MLA_NOPE:]
    k_rope = apply_rope(kr_raw[:, None], cos, sin)
    q_full = jnp.concatenate([q_nope.astype(jnp.float32), apply_rope(q_rope, cos, sin)], axis=-1)
    k_full = jnp.concatenate([k_nope.astype(jnp.float32), jnp.broadcast_to(k_rope, k_nope.shape[:-1] + (MLA_ROPE,))], axis=-1)
    return block_softmax_attention(q_full, k_full, v, (MLA_NOPE + MLA_ROPE) ** -0.5)


def hybrid_mixer(h, w_in, b_forget, norm_cq, norm_ckv, w_uq, w_ukv, head_gain, w_out, rope64, rope32):
    proj = h @ w_in
    (rq, rk, rv, rg, fq, fk, fv, ff, sq, sk, sv, cq, ckv, kr) = jnp.split(proj, np.cumsum(IN_SPLITS)[:-1].tolist(), axis=-1)
    cos64, sin64 = rope64
    cos32, sin32 = rope32
    gw = GROUP_WIDTH
    q_a = apply_rope(split_heads(rq, GROUP_HEADS), cos64, sin64)
    k_a = apply_rope(split_heads(rk, GROUP_HEADS), cos64, sin64) * HEAD_DIM ** -0.5
    o_a = retention_chunkwise(q_a, k_a, split_heads(rv, GROUP_HEADS))
    y_a = head_norm(o_a, head_gain[0:gw], True) * jax.nn.silu(rg.astype(jnp.float32))
    log_f = jax.nn.log_sigmoid(ff.astype(jnp.float32) + b_forget.astype(jnp.float32))
    fcum = jnp.cumsum(log_f, axis=1).transpose(0, 2, 1)
    o_b = block_softmax_attention(split_heads(fq, GROUP_HEADS), split_heads(fk, GROUP_HEADS), split_heads(fv, GROUP_HEADS), HEAD_DIM ** -0.5, fcum)
    y_b = head_norm(o_b, head_gain[gw:2 * gw], False)
    o_c = stick_breaking_attention(split_heads(sq, GROUP_HEADS), split_heads(sk, GROUP_HEADS), split_heads(sv, GROUP_HEADS), HEAD_DIM ** -0.5)
    y_c = head_norm(o_c, head_gain[2 * gw:3 * gw], False)
    o_d = latent_attention(cq, ckv, kr, norm_cq, norm_ckv, w_uq, w_ukv, cos32, sin32)
    y_d = head_norm(o_d, head_gain[3 * gw:4 * gw], False)
    y = jnp.concatenate([y_a, y_b, y_c, y_d], axis=-1).astype(h.dtype)
    return y @ w_out


def peer_ffn(h, w_query, sub_keys, expert_u, expert_v):
    b, s, d = h.shape
    t = b * s
    ht = h.reshape(t, d)
    q = (ht @ w_query).astype(jnp.float32).reshape(t, PEER_HEADS, 2, PEER_HALF)
    scores = jnp.einsum('thpc,hpnc->thpn', q, sub_keys.astype(jnp.float32))
    top_s, top_i = lax.top_k(scores, PEER_TOPK)
    kk = PEER_TOPK * PEER_TOPK
    cand_s = (top_s[:, :, 0, :, None] + top_s[:, :, 1, None, :]).reshape(t, PEER_HEADS, kk)
    cand_e = (top_i[:, :, 0, :, None] * PEER_KEYS + top_i[:, :, 1, None, :]).reshape(t, PEER_HEADS, kk)
    best_s, best_pos = lax.top_k(cand_s, PEER_TOPK)
    experts = jnp.take_along_axis(cand_e, best_pos, axis=-1)
    gates = jax.nn.softmax(best_s, axis=-1)
    n = t // PEER_CHUNK
    xs = (ht.reshape(n, PEER_CHUNK, d),
          experts.reshape(n, PEER_CHUNK, PEER_HEADS * PEER_TOPK),
          gates.reshape(n, PEER_CHUNK, PEER_HEADS * PEER_TOPK))

    def one_chunk(args):
        xi, ei, gi = args
        act = jax.nn.gelu(jnp.einsum('cd,ckd->ck', xi, expert_u[ei]).astype(jnp.float32), approximate=False)
        return jnp.einsum('ck,ckd->cd', (gi * act).astype(xi.dtype), expert_v[ei])

    return lax.map(one_chunk, xs).reshape(b, s, d)


def setup_inputs(seed: int = 0) -> dict:
    key = jax.random.key(seed)
    ks = jax.random.split(key, 16)
    f32 = jnp.float32

    def normal(k, shape, scale):
        return jax.random.normal(k, shape, f32) * scale

    def gain(k, shape):
        return 1.0 + 0.02 * jax.random.normal(k, shape, f32)

    return {
        'x': normal(ks[0], (BATCH, SEQ, D_MODEL), 1.0),
        'norm_mix': gain(ks[1], (DEPTH, D_MODEL)),
        'w_in': normal(ks[2], (DEPTH, D_MODEL, D_IN), D_MODEL ** -0.5),
        'b_forget': FORGET_BIAS_INIT + 0.5 * jax.random.normal(ks[3], (DEPTH, GROUP_HEADS), f32),
        'norm_cq': gain(ks[4], (DEPTH, MLA_Q_RANK)),
        'norm_ckv': gain(ks[5], (DEPTH, MLA_KV_RANK)),
        'w_uq': normal(ks[6], (DEPTH, MLA_Q_RANK, GROUP_HEADS * (MLA_NOPE + MLA_ROPE)), MLA_Q_RANK ** -0.5),
        'w_ukv': normal(ks[7], (DEPTH, MLA_KV_RANK, GROUP_HEADS * (MLA_NOPE + MLA_V)), MLA_KV_RANK ** -0.5),
        'head_gain': gain(ks[8], (DEPTH, D_MIX)),
        'w_out': normal(ks[9], (DEPTH, D_MIX, D_MODEL), D_MIX ** -0.5),
        'norm_ffn': gain(ks[10], (DEPTH, D_MODEL)),
        'w_query': normal(ks[11], (DEPTH, D_MODEL, PEER_HEADS * PEER_QDIM), D_MODEL ** -0.5),
        'sub_keys': normal(ks[12], (DEPTH, PEER_HEADS, 2, PEER_KEYS, PEER_HALF), PEER_HALF ** -0.5),
        'expert_u': normal(ks[13], (DEPTH, PEER_EXPERTS, D_MODEL), D_MODEL ** -0.5),
        'expert_v': normal(ks[14], (DEPTH, PEER_EXPERTS, D_MODEL), PEER_HEADS ** -0.5),
        'norm_final': gain(ks[15], (D_MODEL,)),
    }


def reference(x, norm_mix, w_in, b_forget, norm_cq, norm_ckv, w_uq, w_ukv, head_gain, w_out,
              norm_ffn, w_query, sub_keys, expert_u, expert_v, norm_final):
    seq = x.shape[1]
    rope64 = rope_tables(seq, HEAD_DIM)
    rope32 = rope_tables(seq, MLA_ROPE)
    for l in range(DEPTH):
        h = rms_norm(x, norm_mix[l])
        x = x + hybrid_mixer(h, w_in[l], b_forget[l], norm_cq[l], norm_ckv[l], w_uq[l], w_ukv[l],
                             head_gain[l], w_out[l], rope64, rope32)
        h = rms_norm(x, norm_ffn[l])
        x = x + peer_ffn(h, w_query[l], sub_keys[l], expert_u[l], expert_v[l])
    return rms_norm(x, norm_final)
```

```python
import functools
import math

import numpy as np
import jax
import jax.numpy as jnp
from jax import lax
from jax.experimental import pallas as pl
from jax.experimental.pallas import tpu as pltpu

F32 = jnp.float32
BF16 = jnp.bfloat16

N_HEADS = 4
HEAD_DIM = 64
GROUP_WIDTH = N_HEADS * HEAD_DIM
MLA_NOPE = 64
MLA_ROPE = 32
ROPE_BASE = 10000.0
PEER_HEADS = 8
PEER_KEYS = 128
PEER_TOPK = 16
EPS = 1e-6
NEG_INF = float("-inf")

LANES = 128
VMEM_LIMIT = 56 * 1024 * 1024

PEER_PAIRS = tuple((a, b) for a in range(PEER_TOPK) for b in range(PEER_TOPK // (a + 1)))
PEER_PAIR_ROWS = -(-len(PEER_PAIRS) // 8) * 8

_NT = (((1,), (1,)), ((), ()))
_TN = (((0,), (0,)), ((), ()))


def _dot(a, b):
    return jnp.dot(a, b, preferred_element_type=F32)


def _dot_nt(a, b):
    return lax.dot_general(a, b, _NT, preferred_element_type=F32)


def _dot_tn(a, b):
    return lax.dot_general(a, b, _TN, preferred_element_type=F32)


def _split_dot(x, m, terms):
    acc = None
    r = x
    for t in range(terms):
        p = r.astype(BF16)
        d = _dot(p, m)
        acc = d if acc is None else acc + d
        if t + 1 < terms:
            r = r - p.astype(F32)
    return acc


def _log_sigmoid(x):
    return jnp.minimum(x, 0.0) - jnp.log1p(jnp.exp(-jnp.abs(x)))


def _rms(x, g):
    return x * lax.rsqrt(jnp.mean(x * x, axis=-1, keepdims=True) + EPS) * g


def _rope(x, c, s_signed, half):
    outs = []
    for o in range(0, x.shape[1], LANES):
        xs = x[:, o:o + LANES]
        lane = lax.broadcasted_iota(jnp.int32, xs.shape, 1)
        first = (lane & (2 * half - 1)) < half
        rot = jnp.where(first, pltpu.roll(xs, LANES - half, 1), pltpu.roll(xs, half, 1))
        outs.append(xs * c[:, o:o + LANES] + rot * s_signed[:, o:o + LANES])
    return outs[0] if len(outs) == 1 else jnp.concatenate(outs, axis=1)


def _head_of_lane(width):
    lane = lax.broadcasted_iota(jnp.int32, (1, width), 1)
    if width == GROUP_WIDTH:
        return lane >> 6
    return jnp.where(lane < GROUP_WIDTH, lane >> 6, (lane - GROUP_WIDTH) >> 5)


def _params(semantics):
    return pltpu.CompilerParams(dimension_semantics=semantics, vmem_limit_bytes=VMEM_LIMIT)


def _full(shape):
    nd = len(shape)
    return pl.BlockSpec(shape, lambda *_: (0,) * nd)


def _inproj_body(x_ref, g_ref, w_ref, wff_ref, bf_ref, ncq_ref, nckv_ref, wuq_ref, wukv_ref,
                 c64_ref, s64_ref, c32_ref, s32_ref, tri_ref,
                 qa_ref, ka_ref, va_ref, ga_ref, fq_ref, fk_ref, fv_ref, sq_ref, sk_ref, sv_ref,
                 qd_ref, kd_ref, vd_ref, fc_ref, carry_ref, *, tiles_per_seq):
    i = pl.program_id(0)
    x = x_ref[...]
    hb = _rms(x, g_ref[...]).astype(BF16)
    gw = GROUP_WIDTH

    def proj(k, n=gw):
        return _dot(hb, w_ref[:, k * gw:k * gw + n])

    c64, s64 = c64_ref[...], s64_ref[...]
    c32, s32 = c32_ref[...], s32_ref[...]
    scale = HEAD_DIM ** -0.5
    qa_ref[...] = _rope(proj(0), c64, s64, HEAD_DIM // 2).astype(BF16)
    ka_ref[...] = (_rope(proj(1), c64, s64, HEAD_DIM // 2) * scale).astype(BF16)
    va_ref[...] = proj(2).astype(BF16)
    ga_ref[...] = proj(3).astype(BF16)
    fq_ref[...] = (proj(4) * scale).astype(BF16)
    fk_ref[...] = proj(5).astype(BF16)
    fv_ref[...] = proj(6).astype(BF16)
    sq_ref[...] = (proj(7) * scale).astype(BF16)
    sk_ref[...] = proj(8).astype(BF16)
    sv_ref[...] = proj(9).astype(BF16)

    mla_scale = (MLA_NOPE + MLA_ROPE) ** -0.5
    cq = _rms(proj(10), ncq_ref[...]).astype(BF16)
    qd_ref[:, 0:gw] = (_dot(cq, wuq_ref[:, 0:gw]) * mla_scale).astype(BF16)
    qr = _rope(_dot(cq, wuq_ref[:, gw:gw + LANES]), c32, s32, MLA_ROPE // 2)
    qd_ref[:, gw:gw + LANES] = (qr * mla_scale).astype(BF16)
    ckv = _rms(proj(11, LANES), nckv_ref[...]).astype(BF16)
    kd_ref[:, 0:gw] = _dot(ckv, wukv_ref[:, 0:gw]).astype(BF16)
    vd_ref[...] = _dot(ckv, wukv_ref[:, gw:2 * gw]).astype(BF16)
    kr = _dot(hb, w_ref[:, 11 * gw + LANES:11 * gw + 2 * LANES])
    kd_ref[:, gw:gw + LANES] = _rope(kr, c32, s32, MLA_ROPE // 2).astype(BF16)

    lf = _log_sigmoid(_dot_nt(wff_ref[...], hb) + bf_ref[...])
    cum = _split_dot(lf, tri_ref[...], 3)

    @pl.when(i % tiles_per_seq == 0)
    def _():
        carry_ref[...] = jnp.zeros_like(carry_ref)

    fc = cum + carry_ref[:, 0:1]
    fc_ref[...] = fc
    tm = fc.shape[1]
    carry_ref[...] = jnp.broadcast_to(fc[:, tm - 1:tm], carry_ref.shape)


def _inproj(x2, g, w_main, wff_t, bf, ncq, nckv, wuq, wukv, c64, s64, c32, s32, seq, tm):
    t, d = x2.shape
    gw = GROUP_WIDTH
    tiles_per_seq = seq // tm
    tri = (jnp.arange(tm)[:, None] <= jnp.arange(tm)[None, :]).astype(BF16)
    tok = lambda w: pl.BlockSpec((tm, w), lambda i: (i, 0))
    pos = lambda w: pl.BlockSpec((tm, w), lambda i: (i % tiles_per_seq, 0))
    out_widths = [gw] * 10 + [gw + LANES, gw + LANES, gw]
    out_shape = [jax.ShapeDtypeStruct((t, w), BF16) for w in out_widths]
    out_shape.append(jax.ShapeDtypeStruct((8, t), F32))
    out_specs = [tok(w) for w in out_widths] + [pl.BlockSpec((8, tm), lambda i: (0, i))]
    return pl.pallas_call(
        functools.partial(_inproj_body, tiles_per_seq=tiles_per_seq),
        grid=(t // tm,),
        in_specs=[tok(d), _full(g.shape), _full(w_main.shape), _full(wff_t.shape), _full(bf.shape),
                  _full(ncq.shape), _full(nckv.shape), _full(wuq.shape), _full(wukv.shape),
                  pos(gw), pos(gw), pos(LANES), pos(LANES), _full(tri.shape)],
        out_specs=out_specs,
        out_shape=out_shape,
        scratch_shapes=[pltpu.VMEM((8, LANES), F32)],
        compiler_params=_params(("arbitrary",)),
        name="inproj",
    )(x2, g, w_main, wff_t, bf, ncq, nckv, wuq, wukv, c64, s64, c32, s32, tri)


def _group_mean(x, bd):
    return _split_dot(x, bd, 2)


def _retention_body(q_ref, k_ref, v_ref, g_ref, gain_ref, bd_ref, o_ref, state_ref, *, chunk):
    ci = pl.program_id(1)

    @pl.when(ci == 0)
    def _():
        state_ref[...] = jnp.zeros_like(state_ref)

    gw = GROUP_WIDTH
    log_gamma = [math.log1p(-(2.0 ** (-5.0 - h))) for h in range(N_HEADS)]
    hl = _head_of_lane(gw)
    lg = jnp.zeros((1, gw), F32)
    for h in range(N_HEADS):
        lg = jnp.where(hl == h, log_gamma[h], lg)

    q, k, v = q_ref[...], k_ref[...], v_ref[...]
    ri = lax.broadcasted_iota(jnp.int32, (chunk, chunk), 0)
    cj = lax.broadcasted_iota(jnp.int32, (chunk, chunk), 1)
    rel = jnp.maximum(ri - cj, 0).astype(F32)
    causal = ri >= cj
    o = jnp.zeros((chunk, gw), F32)
    for h in range(N_HEADS):
        qm = jnp.where(hl == h, q, jnp.zeros_like(q))
        inner = _dot_nt(qm, k) * jnp.where(causal, jnp.exp(log_gamma[h] * rel), 0.0)
        o = o + jnp.where(hl == h, _dot(inner.astype(BF16), v), 0.0)

    pos = lax.broadcasted_iota(jnp.int32, (chunk, 1), 0).astype(F32)
    st = state_ref[...]
    q_dec = (q.astype(F32) * jnp.exp(lg * (pos + 1.0))).astype(BF16)
    o = o + _dot(q_dec, st.astype(BF16))
    k_dec = (k.astype(F32) * jnp.exp(lg * (chunk - 1.0 - pos))).astype(BF16)
    r2 = lax.broadcasted_iota(jnp.int32, (gw, gw), 0) >> 6
    c2 = lax.broadcasted_iota(jnp.int32, (gw, gw), 1) >> 6
    state_ref[...] = jnp.where(r2 == c2, st * jnp.exp(lg * float(chunk)) + _dot_tn(k_dec, v), 0.0)

    bd = bd_ref[...]
    oc = o - _group_mean(o, bd)
    y = oc * lax.rsqrt(_group_mean(oc * oc, bd) + EPS) * gain_ref[...]
    g = g_ref[...].astype(F32)
    o_ref[...] = (y * (g / (1.0 + jnp.exp(-g)))).astype(BF16)


def _retention(qa, ka, va, ga, gain, bd, batch, seq, chunk):
    t, gw = qa.shape
    n = seq // chunk
    blk = pl.BlockSpec((chunk, gw), lambda b, c: (b * n + c, 0))
    return pl.pallas_call(
        functools.partial(_retention_body, chunk=chunk),
        grid=(batch, n),
        in_specs=[blk, blk, blk, blk, _full(gain.shape), _full(bd.shape)],
        out_specs=blk,
        out_shape=jax.ShapeDtypeStruct((t, gw), BF16),
        scratch_shapes=[pltpu.VMEM((gw, gw), F32)],
        compiler_params=_params(("arbitrary", "arbitrary")),
        name="retention",
    )(qa, ka, va, ga, gain, bd)


def _mask_queries(q_ref, qm_ref):
    q = q_ref[...]
    hq = _head_of_lane(q.shape[1])
    for h in range(N_HEADS):
        qm_ref[h] = jnp.where(hq == h, q, jnp.zeros_like(q))


def _combine_heads(acc_ref, scale_of_head):
    hv = _head_of_lane(GROUP_WIDTH)
    o = jnp.zeros(acc_ref.shape[1:], F32)
    for h in range(N_HEADS):
        o = o + jnp.where(hv == h, acc_ref[h] * scale_of_head(h), 0.0)
    return o


def _head_rms(o, bd, gain):
    return o * lax.rsqrt(_group_mean(o * o, bd) + EPS) * gain


def _softmax_attn_body(*refs, tq, has_bias):
    if has_bias:
        q_ref, k_ref, v_ref, bias_ref, gain_ref, bd_ref, o_ref, qm_ref, m_ref, l_ref, acc_ref = refs
    else:
        q_ref, k_ref, v_ref, gain_ref, bd_ref, o_ref, qm_ref, m_ref, l_ref, acc_ref = refs
        bias_ref = None
    qi = pl.program_id(1)
    _mask_queries(q_ref, qm_ref)
    m_ref[...] = jnp.full_like(m_ref, NEG_INF)
    l_ref[...] = jnp.zeros_like(l_ref)
    acc_ref[...] = jnp.zeros_like(acc_ref)
    row = lax.broadcasted_iota(jnp.int32, (tq, tq), 0)
    col = lax.broadcasted_iota(jnp.int32, (tq, tq), 1)

    def block(j, diagonal):
        off = pl.multiple_of(j * tq, tq)
        kb = k_ref[pl.ds(off, tq), :]
        vb = v_ref[pl.ds(off, tq), :]
        for h in range(N_HEADS):
            s = _dot_nt(qm_ref[h], kb)
            if has_bias:
                s = s - bias_ref[h:h + 1, pl.ds(off, tq)]
            if diagonal:
                s = jnp.where(col <= row, s, NEG_INF)
            m_prev = m_ref[h]
            m_new = jnp.maximum(m_prev, jnp.max(s, axis=1, keepdims=True))
            alpha = jnp.exp(m_prev - m_new)
            p = jnp.exp(s - m_new)
            l_ref[h] = alpha * l_ref[h] + jnp.sum(p, axis=1, keepdims=True)
            acc_ref[h] = alpha * acc_ref[h] + _dot(p.astype(BF16), vb)
            m_ref[h] = m_new

    def body(j, carry):
        block(j, False)
        return carry

    lax.fori_loop(0, qi, body, 0)
    block(qi, True)
    o = _combine_heads(acc_ref, lambda h: 1.0 / l_ref[h])
    o_ref[...] = _head_rms(o, bd_ref[...], gain_ref[...]).astype(BF16)


def _softmax_attn(q, k, v, bias_t, gain, bd, batch, seq, tq):
    t, wq = q.shape
    gw = GROUP_WIDTH
    nq = seq // tq
    has_bias = bias_t is not None
    in_specs = [pl.BlockSpec((tq, wq), lambda b, i: (b * nq + i, 0)),
                pl.BlockSpec((seq, wq), lambda b, i: (b, 0)),
                pl.BlockSpec((seq, gw), lambda b, i: (b, 0))]
    args = [q, k, v]
    if has_bias:
        in_specs.append(pl.BlockSpec((8, seq), lambda b, i: (0, b)))
        args.append(bias_t)
    in_specs += [_full(gain.shape), _full(bd.shape)]
    args += [gain, bd]
    return pl.pallas_call(
        functools.partial(_softmax_attn_body, tq=tq, has_bias=has_bias),
        grid=(batch, nq),
        in_specs=in_specs,
        out_specs=pl.BlockSpec((tq, gw), lambda b, i: (b * nq + i, 0)),
        out_shape=jax.ShapeDtypeStruct((t, gw), BF16),
        scratch_shapes=[pltpu.VMEM((N_HEADS, tq, wq), BF16),
                        pltpu.VMEM((N_HEADS, tq, 1), F32),
                        pltpu.VMEM((N_HEADS, tq, 1), F32),
                        pltpu.VMEM((N_HEADS, tq, gw), F32)],
        compiler_params=_params(("arbitrary", "arbitrary")),
        name="fox_attn" if has_bias else "mla_attn",
    )(*args)


def _stickbreak_body(q_ref, k_ref, v_ref, gain_ref, bd_ref, tri_ref, o_ref,
                     qm_ref, carry_ref, acc_ref, *, tq):
    qi = pl.program_id(1)
    _mask_queries(q_ref, qm_ref)
    carry_ref[...] = jnp.zeros_like(carry_ref)
    acc_ref[...] = jnp.zeros_like(acc_ref)
    row = lax.broadcasted_iota(jnp.int32, (tq, tq), 0)
    col = lax.broadcasted_iota(jnp.int32, (tq, tq), 1)

    def block(j, diagonal):
        off = pl.multiple_of(j * tq, tq)
        kb = k_ref[pl.ds(off, tq), :]
        vb = v_ref[pl.ds(off, tq), :]
        tri = tri_ref[...]
        for h in range(N_HEADS):
            z = _dot_nt(qm_ref[h], kb)
            log_beta = _log_sigmoid(z)
            log_rest = log_beta - z
            if diagonal:
                log_rest = jnp.where(col < row, log_rest, 0.0)
            after = _split_dot(log_rest, tri, 2) + carry_ref[h]
            w = jnp.exp(log_beta + after)
            if diagonal:
                w = jnp.where(col < row, w, 0.0)
            acc_ref[h] = acc_ref[h] + _dot(w.astype(BF16), vb)
            carry_ref[h] = carry_ref[h] + jnp.sum(log_rest, axis=1, keepdims=True)

    block(qi, True)

    def body(it, carry):
        block(qi - 1 - it, False)
        return carry

    lax.fori_loop(0, qi, body, 0)
    o = _combine_heads(acc_ref, lambda h: 1.0)
    o_ref[...] = _head_rms(o, bd_ref[...], gain_ref[...]).astype(BF16)


def _stickbreak(q, k, v, gain, bd, batch, seq, tq):
    t, gw = q.shape
    nq = seq // tq
    tri = (jnp.arange(tq)[:, None] > jnp.arange(tq)[None, :]).astype(BF16)
    return pl.pallas_call(
        functools.partial(_stickbreak_body, tq=tq),
        grid=(batch, nq),
        in_specs=[pl.BlockSpec((tq, gw), lambda b, i: (b * nq + i, 0)),
                  pl.BlockSpec((seq, gw), lambda b, i: (b, 0)),
                  pl.BlockSpec((seq, gw), lambda b, i: (b, 0)),
                  _full(gain.shape), _full(bd.shape), _full(tri.shape)],
        out_specs=pl.BlockSpec((tq, gw), lambda b, i: (b * nq + i, 0)),
        out_shape=jax.ShapeDtypeStruct((t, gw), BF16),
        scratch_shapes=[pltpu.VMEM((N_HEADS, tq, gw), BF16),
                        pltpu.VMEM((N_HEADS, tq, 1), F32),
                        pltpu.VMEM((N_HEADS, tq, gw), F32)],
        compiler_params=_params(("arbitrary", "arbitrary")),
        name="stickbreak_attn",
    )(q, k, v, gain, bd, tri)


def _outproj_body(x_ref, ya_ref, yb_ref, yc_ref, yd_ref, wo_ref, gf_ref, wq_ref, sk_ref,
                  x1_ref, h2_ref, sc_ref):
    gw = GROUP_WIDTH
    acc = x_ref[...]
    for n, y_ref in enumerate((ya_ref, yb_ref, yc_ref, yd_ref)):
        acc = acc + _dot(y_ref[...], wo_ref[n * gw:(n + 1) * gw, :])
    x1_ref[...] = acc
    h2 = _rms(acc, gf_ref[...]).astype(BF16)
    h2_ref[...] = h2
    q = _dot(h2, wq_ref[...])
    for hp in range(sc_ref.shape[0]):
        sc_ref[hp] = _dot_nt(sk_ref[hp], q[:, hp * LANES:(hp + 1) * LANES].astype(BF16))


def _outproj(x2, ys, wo, gf, wq, sk, tm):
    t, d = x2.shape
    gw = GROUP_WIDTH
    nk = sk.shape[0]
    tok = lambda w: pl.BlockSpec((tm, w), lambda i: (i, 0))
    return pl.pallas_call(
        _outproj_body,
        grid=(t // tm,),
        in_specs=[tok(d)] + [tok(gw)] * 4 + [_full(wo.shape), _full(gf.shape), _full(wq.shape), _full(sk.shape)],
        out_specs=[tok(d), tok(d), pl.BlockSpec((nk, PEER_KEYS, tm), lambda i: (0, 0, i))],
        out_shape=[jax.ShapeDtypeStruct((t, d), F32), jax.ShapeDtypeStruct((t, d), BF16),
                   jax.ShapeDtypeStruct((nk, PEER_KEYS, t), F32)],
        compiler_params=_params(("arbitrary",)),
        name="outproj_peerq",
    )(x2, *ys, wo, gf, wq, sk)


def _top_k_rows(s, k, big):
    rows = lax.broadcasted_iota(jnp.int32, s.shape, 0)
    rank = jnp.full(s.shape, big, F32)
    vals = []
    for r in range(k):
        m = jnp.max(s, axis=0, keepdims=True)
        idx = jnp.min(jnp.where(s == m, rows, s.shape[0]), axis=0, keepdims=True)
        hit = rows == idx
        s = jnp.where(hit, NEG_INF, s)
        rank = jnp.where(hit, float(r), rank)
        vals.append(m)
    return vals, rank


def _route_body(sc_ref, rank2_ref, cnt_ref, e1_ref, e2_ref, cand_ref):
    k = PEER_TOPK
    not_selected = float(2 * k)

    def head(h, carry):
        s1 = sc_ref[2 * h]
        s2 = sc_ref[2 * h + 1]
        v1, rank1 = _top_k_rows(s1, k, not_selected)
        v2, rank2 = _top_k_rows(s2, k, not_selected)
        cand_ref[...] = jnp.full(cand_ref.shape, NEG_INF, F32)
        for p, (a, b) in enumerate(PEER_PAIRS):
            cand_ref[p:p + 1, :] = v1[a] + v2[b]
        cand = cand_ref[...]
        _, crank = _top_k_rows(cand, k, not_selected)
        sel = crank < float(k)
        top = v1[0] + v2[0]
        z = jnp.sum(jnp.where(sel, jnp.exp(cand - top), 0.0), axis=0, keepdims=True)
        prow = lax.broadcasted_iota(jnp.int32, cand.shape, 0)
        cnt = jnp.zeros(s1.shape, F32)
        row0 = 0
        for a in range(k):
            na = k // (a + 1)
            in_a = (prow >= row0) & (prow < row0 + na) & sel
            n_a = jnp.sum(jnp.where(in_a, 1.0, 0.0), axis=0, keepdims=True)
            cnt = cnt + jnp.where(rank1 == float(a), n_a, 0.0)
            row0 += na
        rank2_ref[h] = rank2
        cnt_ref[h] = cnt
        e1_ref[h] = jnp.exp(s1 - v1[0]) / z
        e2_ref[h] = jnp.exp(s2 - v2[0])
        return carry

    lax.fori_loop(0, PEER_HEADS, head, 0)


def _route(scores_t, tr):
    nk, nkeys, t = scores_t.shape
    out = jax.ShapeDtypeStruct((PEER_HEADS, nkeys, t), F32)
    spec = pl.BlockSpec((PEER_HEADS, nkeys, tr), lambda i: (0, 0, i))
    return pl.pallas_call(
        _route_body,
        grid=(t // tr,),
        in_specs=[pl.BlockSpec((nk, nkeys, tr), lambda i: (0, 0, i))],
        out_specs=[spec] * 4,
        out_shape=[out] * 4,
        scratch_shapes=[pltpu.VMEM((PEER_PAIR_ROWS, tr), F32)],
        compiler_params=_params(("arbitrary",)),
        name="peer_route",
    )(scores_t)


def _peer_body(h_ref, u_ref, vt_ref, rank2_ref, cnt_ref, e1_ref, e2_ref, x1_ref, gfin_ref, o_ref,
               acc_ref, s_ref, w_ref, *, groups, final_norm):
    si = pl.program_id(1)

    @pl.when(si == 0)
    def _():
        acc_ref[...] = jnp.zeros_like(acc_ref)

    s_ref[...] = _dot_nt(u_ref[...], h_ref[...])
    nk = PEER_KEYS
    for g in range(groups):
        i = si * groups + g
        s = s_ref[g * nk:(g + 1) * nk, :]
        act = 0.5 * s * (1.0 + lax.erf(s * (2.0 ** -0.5)))
        gate = jnp.zeros_like(s)
        for h in range(PEER_HEADS):
            chosen = rank2_ref[h] < cnt_ref[h, pl.ds(i, 1), :]
            gate = gate + jnp.where(chosen, e2_ref[h], 0.0) * e1_ref[h, pl.ds(i, 1), :]
        w_ref[g * nk:(g + 1) * nk, :] = (gate * act).astype(BF16)
    acc_ref[...] += _dot(vt_ref[...], w_ref[...])

    @pl.when(si == pl.num_programs(1) - 1)
    def _():
        y = x1_ref[...] + acc_ref[...].T
        if final_norm:
            y = _rms(y, gfin_ref[...])
        o_ref[...] = y


def _peer(h2, u, vt, rank2, cnt, e1, e2, x1, gfin, final_norm, tm, groups):
    t, d = h2.shape
    ne = u.shape[0]
    slab = groups * PEER_KEYS
    tok = pl.BlockSpec((tm, d), lambda i, s: (i, 0))
    route = pl.BlockSpec((PEER_HEADS, PEER_KEYS, tm), lambda i, s: (0, 0, i))
    return pl.pallas_call(
        functools.partial(_peer_body, groups=groups, final_norm=final_norm),
        grid=(t // tm, ne // slab),
        in_specs=[tok, pl.BlockSpec((slab, d), lambda i, s: (s, 0)),
                  pl.BlockSpec((d, slab), lambda i, s: (0, s)),
                  route, route, route, route, tok, _full(gfin.shape)],
        out_specs=tok,
        out_shape=jax.ShapeDtypeStruct((t, d), F32),
        scratch_shapes=[pltpu.VMEM((d, tm), F32), pltpu.VMEM((slab, tm), F32), pltpu.VMEM((slab, tm), BF16)],
        compiler_params=_params(("arbitrary", "arbitrary")),
        name="peer_experts",
    )(h2, u, vt, rank2, cnt, e1, e2, x1, gfin)


def _rope_tables(seq, dim, reps):
    pos = jnp.arange(seq, dtype=F32)
    inv = ROPE_BASE ** (-jnp.arange(0, dim, 2, dtype=F32) / dim)
    ang = pos[:, None] * inv[None, :]
    cos, sin = jnp.cos(ang), jnp.sin(ang)
    return (jnp.tile(jnp.concatenate([cos, cos], axis=1), (1, reps)),
            jnp.tile(jnp.concatenate([-sin, sin], axis=1), (1, reps)))


def _tile_sizes(seq, tokens):
    pick = lambda n, pref: pref if n % pref == 0 else n
    return dict(tm=pick(seq, 512), chunk=pick(seq, 256), tq=pick(seq, 256),
                tr=pick(tokens, 256), tp=pick(tokens, 512), groups=4)


def kernel(x, norm_mix, w_in, b_forget, norm_cq, norm_ckv, w_uq, w_ukv, head_gain, w_out, norm_ffn, w_query, sub_keys, expert_u, expert_v, norm_final):
    batch, seq, d = x.shape
    depth = w_in.shape[0]
    t = batch * seq
    gw = GROUP_WIDTH
    ts = _tile_sizes(seq, t)
    c64, s64 = _rope_tables(seq, HEAD_DIM, N_HEADS)
    c32, s32 = _rope_tables(seq, MLA_ROPE, N_HEADS)
    blk = jnp.arange(gw) // HEAD_DIM
    bd = jnp.where(blk[:, None] == blk[None, :], 1.0 / HEAD_DIM, 0.0).astype(BF16)
    row = lambda v: v.reshape(1, -1).astype(F32)

    x2 = x.reshape(t, d)
    for l in range(depth):
        w = w_in[l]
        o_ff = 7 * gw
        o_s = o_ff + N_HEADS
        o_cq = o_s + 3 * gw
        o_ckv = o_cq + w_uq.shape[1]
        o_kr = o_ckv + w_ukv.shape[1]
        w_main = jnp.concatenate(
            [w[:, :o_ff], w[:, o_s:o_kr], jnp.tile(w[:, o_kr:o_kr + MLA_ROPE], (1, N_HEADS))], axis=1).astype(BF16)
        wff_t = jnp.zeros((8, d), F32).at[:N_HEADS].set(w[:, o_ff:o_s].T).astype(BF16)
        bf = jnp.zeros((8, 1), F32).at[:N_HEADS, 0].set(b_forget[l])
        uq = w_uq[l].reshape(-1, N_HEADS, MLA_NOPE + MLA_ROPE)
        wuq = jnp.concatenate([uq[:, :, :MLA_NOPE].reshape(-1, gw),
                               uq[:, :, MLA_NOPE:].reshape(-1, N_HEADS * MLA_ROPE)], axis=1).astype(BF16)
        ukv = w_ukv[l].reshape(-1, N_HEADS, MLA_NOPE + HEAD_DIM)
        wukv = jnp.concatenate([ukv[:, :, :MLA_NOPE].reshape(-1, gw),
                                ukv[:, :, MLA_NOPE:].reshape(-1, gw)], axis=1).astype(BF16)
        (qa, ka, va, ga, fq, fk, fv, sq, sk, sv, qd, kd, vd, fcum_t) = _inproj(
            x2, row(norm_mix[l]), w_main, wff_t, bf, row(norm_cq[l]), row(norm_ckv[l]), wuq, wukv,
            c64, s64, c32, s32, seq, ts["tm"])
        gain = head_gain[l].astype(F32)
        ya = _retention(qa, ka, va, ga, row(gain[0:gw]), bd, batch, seq, ts["chunk"])
        yb = _softmax_attn(fq, fk, fv, fcum_t, row(gain[gw:2 * gw]), bd, batch, seq, ts["tq"])
        yc = _stickbreak(sq, sk, sv, row(gain[2 * gw:3 * gw]), bd, batch, seq, ts["tq"])
        yd = _softmax_attn(qd, kd, vd, None, row(gain[3 * gw:4 * gw]), bd, batch, seq, ts["tq"])
        keys = sub_keys[l].reshape(2 * PEER_HEADS, PEER_KEYS, -1).astype(BF16)
        x1, h2, scores_t = _outproj(x2, (ya, yb, yc, yd), w_out[l].astype(BF16), row(norm_ffn[l]),
                                    w_query[l].astype(BF16), keys, ts["tm"])
        rank2, cnt, e1, e2 = _route(scores_t, ts["tr"])
        x2 = _peer(h2, expert_u[l].astype(BF16), expert_v[l].T.astype(BF16), rank2, cnt, e1, e2, x1,
                   row(norm_final), l == depth - 1, ts["tp"], ts["groups"])
    return x2.reshape(batch, seq, d)
```

```python
import functools
import math

import numpy as np
import jax
import jax.numpy as jnp
from jax import lax
from jax.experimental import pallas as pl
from jax.experimental.pallas import tpu as pltpu

F32 = jnp.float32
BF16 = jnp.bfloat16

N_HEADS = 4
HEAD_DIM = 64
GROUP_WIDTH = N_HEADS * HEAD_DIM
MLA_NOPE = 64
MLA_ROPE = 32
ROPE_BASE = 10000.0
PEER_HEADS = 8
PEER_KEYS = 128
PEER_TOPK = 16
EPS = 1e-6
NEG_INF = float("-inf")

LANES = 128
VMEM_LIMIT = 56 * 1024 * 1024

PEER_PAIRS = tuple((a, b) for a in range(PEER_TOPK) for b in range(PEER_TOPK // (a + 1)))
PEER_PAIR_ROWS = -(-len(PEER_PAIRS) // 8) * 8

_NT = (((1,), (1,)), ((), ()))
_TN = (((0,), (0,)), ((), ()))


def _dot(a, b):
    return jnp.dot(a, b, preferred_element_type=F32)


def _dot_nt(a, b):
    return lax.dot_general(a, b, _NT, preferred_element_type=F32)


def _dot_tn(a, b):
    return lax.dot_general(a, b, _TN, preferred_element_type=F32)


def _split_dot(x, m, terms):
    acc = None
    r = x
    for t in range(terms):
        p = r.astype(BF16)
        d = _dot(p, m)
        acc = d if acc is None else acc + d
        if t + 1 < terms:
            r = r - p.astype(F32)
    return acc


def _log_sigmoid(x):
    return jnp.minimum(x, 0.0) - jnp.log1p(jnp.exp(-jnp.abs(x)))


def _rms(x, g):
    return x * lax.rsqrt(jnp.mean(x * x, axis=-1, keepdims=True) + EPS) * g


def _rope(x, c, s_signed, half):
    outs = []
    for o in range(0, x.shape[1], LANES):
        xs = x[:, o:o + LANES]
        lane = lax.broadcasted_iota(jnp.int32, xs.shape, 1)
        first = (lane & (2 * half - 1)) < half
        rot = jnp.where(first, pltpu.roll(xs, LANES - half, 1), pltpu.roll(xs, half, 1))
        outs.append(xs * c[:, o:o + LANES] + rot * s_signed[:, o:o + LANES])
    return outs[0] if len(outs) == 1 else jnp.concatenate(outs, axis=1)


def _head_of_lane(width):
    lane = lax.broadcasted_iota(jnp.int32, (1, width), 1)
    if width == GROUP_WIDTH:
        return lane >> 6
    return jnp.where(lane < GROUP_WIDTH, lane >> 6, (lane - GROUP_WIDTH) >> 5)


def _params(semantics, flags=None):
    return pltpu.CompilerParams(dimension_semantics=semantics, vmem_limit_bytes=VMEM_LIMIT, flags=flags)


def _full(shape):
    nd = len(shape)
    return pl.BlockSpec(shape, lambda *_: (0,) * nd)


def _inproj_body(x_ref, g_ref, w_ref, wff_ref, bf_ref, ncq_ref, nckv_ref, wuq_ref, wukv_ref,
                 c64_ref, s64_ref, c32_ref, s32_ref, tri_ref,
                 qa_ref, ka_ref, va_ref, ga_ref, fq_ref, fk_ref, fv_ref, sq_ref, sk_ref, sv_ref,
                 qd_ref, kd_ref, vd_ref, fc_ref, carry_ref, *, tiles_per_seq):
    i = pl.program_id(0)
    x = x_ref[...]
    hb = _rms(x, g_ref[...]).astype(BF16)
    gw = GROUP_WIDTH

    def proj(k, n=gw):
        return _dot(hb, w_ref[:, k * gw:k * gw + n])

    c64, s64 = c64_ref[...], s64_ref[...]
    c32, s32 = c32_ref[...], s32_ref[...]
    scale = HEAD_DIM ** -0.5
    qa_ref[...] = _rope(proj(0), c64, s64, HEAD_DIM // 2).astype(BF16)
    ka_ref[...] = (_rope(proj(1), c64, s64, HEAD_DIM // 2) * scale).astype(BF16)
    va_ref[...] = proj(2).astype(BF16)
    ga_ref[...] = proj(3).astype(BF16)
    fq_ref[...] = (proj(4) * scale).astype(BF16)
    fk_ref[...] = proj(5).astype(BF16)
    fv_ref[...] = proj(6).astype(BF16)
    sq_ref[...] = (proj(7) * scale).astype(BF16)
    sk_ref[...] = proj(8).astype(BF16)
    sv_ref[...] = proj(9).astype(BF16)

    mla_scale = (MLA_NOPE + MLA_ROPE) ** -0.5
    cq = _rms(proj(10), ncq_ref[...]).astype(BF16)
    qd_ref[:, 0:gw] = (_dot(cq, wuq_ref[:, 0:gw]) * mla_scale).astype(BF16)
    qr = _rope(_dot(cq, wuq_ref[:, gw:gw + LANES]), c32, s32, MLA_ROPE // 2)
    qd_ref[:, gw:gw + LANES] = (qr * mla_scale).astype(BF16)
    ckv = _rms(proj(11, LANES), nckv_ref[...]).astype(BF16)
    kd_ref[:, 0:gw] = _dot(ckv, wukv_ref[:, 0:gw]).astype(BF16)
    vd_ref[...] = _dot(ckv, wukv_ref[:, gw:2 * gw]).astype(BF16)
    kr = _dot(hb, w_ref[:, 11 * gw + LANES:11 * gw + 2 * LANES])
    kd_ref[:, gw:gw + LANES] = _rope(kr, c32, s32, MLA_ROPE // 2).astype(BF16)

    lf = _log_sigmoid(_dot_nt(wff_ref[...], hb) + bf_ref[...])
    cum = _split_dot(lf, tri_ref[...], 3)

    @pl.when(i % tiles_per_seq == 0)
    def _():
        carry_ref[...] = jnp.zeros_like(carry_ref)

    fc = cum + carry_ref[:, 0:1]
    fc_ref[...] = fc
    tm = fc.shape[1]
    carry_ref[...] = jnp.broadcast_to(fc[:, tm - 1:tm], carry_ref.shape)


def _inproj(x2, g, w_main, wff_t, bf, ncq, nckv, wuq, wukv, c64, s64, c32, s32, seq, tm):
    t, d = x2.shape
    gw = GROUP_WIDTH
    tiles_per_seq = seq // tm
    tri = (jnp.arange(tm)[:, None] <= jnp.arange(tm)[None, :]).astype(BF16)
    tok = lambda w: pl.BlockSpec((tm, w), lambda i: (i, 0))
    pos = lambda w: pl.BlockSpec((tm, w), lambda i: (i % tiles_per_seq, 0))
    out_widths = [gw] * 10 + [gw + LANES, gw + LANES, gw]
    out_shape = [jax.ShapeDtypeStruct((t, w), BF16) for w in out_widths]
    out_shape.append(jax.ShapeDtypeStruct((8, t), F32))
    out_specs = [tok(w) for w in out_widths] + [pl.BlockSpec((8, tm), lambda i: (0, i))]
    return pl.pallas_call(
        functools.partial(_inproj_body, tiles_per_seq=tiles_per_seq),
        grid=(t // tm,),
        in_specs=[tok(d), _full(g.shape), _full(w_main.shape), _full(wff_t.shape), _full(bf.shape),
                  _full(ncq.shape), _full(nckv.shape), _full(wuq.shape), _full(wukv.shape),
                  pos(gw), pos(gw), pos(LANES), pos(LANES), _full(tri.shape)],
        out_specs=out_specs,
        out_shape=out_shape,
        scratch_shapes=[pltpu.VMEM((8, LANES), F32)],
        compiler_params=_params(("arbitrary",)),
        name="inproj",
    )(x2, g, w_main, wff_t, bf, ncq, nckv, wuq, wukv, c64, s64, c32, s32, tri)


def _group_mean(x, bd):
    return _split_dot(x, bd, 2)


def _retention_body(q_ref, k_ref, v_ref, g_ref, gain_ref, bd_ref, o_ref, state_ref, *, chunk):
    ci = pl.program_id(1)

    @pl.when(ci == 0)
    def _():
        state_ref[...] = jnp.zeros_like(state_ref)

    gw = GROUP_WIDTH
    log_gamma = [math.log1p(-(2.0 ** (-5.0 - h))) for h in range(N_HEADS)]
    hl = _head_of_lane(gw)
    lg = jnp.zeros((1, gw), F32)
    for h in range(N_HEADS):
        lg = jnp.where(hl == h, log_gamma[h], lg)

    q, k, v = q_ref[...], k_ref[...], v_ref[...]
    ri = lax.broadcasted_iota(jnp.int32, (chunk, chunk), 0)
    cj = lax.broadcasted_iota(jnp.int32, (chunk, chunk), 1)
    rel = jnp.maximum(ri - cj, 0).astype(F32)
    causal = ri >= cj
    o = jnp.zeros((chunk, gw), F32)
    for h in range(N_HEADS):
        qm = jnp.where(hl == h, q, jnp.zeros_like(q))
        inner = _dot_nt(qm, k) * jnp.where(causal, jnp.exp(log_gamma[h] * rel), 0.0)
        o = o + jnp.where(hl == h, _dot(inner.astype(BF16), v), 0.0)

    pos = lax.broadcasted_iota(jnp.int32, (chunk, 1), 0).astype(F32)
    st = state_ref[...]
    q_dec = (q.astype(F32) * jnp.exp(lg * (pos + 1.0))).astype(BF16)
    o = o + _dot(q_dec, st.astype(BF16))
    k_dec = (k.astype(F32) * jnp.exp(lg * (chunk - 1.0 - pos))).astype(BF16)
    r2 = lax.broadcasted_iota(jnp.int32, (gw, gw), 0) >> 6
    c2 = lax.broadcasted_iota(jnp.int32, (gw, gw), 1) >> 6
    state_ref[...] = jnp.where(r2 == c2, st * jnp.exp(lg * float(chunk)) + _dot_tn(k_dec, v), 0.0)

    bd = bd_ref[...]
    oc = o - _group_mean(o, bd)
    y = oc * lax.rsqrt(_group_mean(oc * oc, bd) + EPS) * gain_ref[...]
    g = g_ref[...].astype(F32)
    o_ref[...] = (y * (g / (1.0 + jnp.exp(-g)))).astype(BF16)


def _retention(qa, ka, va, ga, gain, bd, batch, seq, chunk):
    t, gw = qa.shape
    n = seq // chunk
    blk = pl.BlockSpec((chunk, gw), lambda b, c: (b * n + c, 0))
    return pl.pallas_call(
        functools.partial(_retention_body, chunk=chunk),
        grid=(batch, n),
        in_specs=[blk, blk, blk, blk, _full(gain.shape), _full(bd.shape)],
        out_specs=blk,
        out_shape=jax.ShapeDtypeStruct((t, gw), BF16),
        scratch_shapes=[pltpu.VMEM((gw, gw), F32)],
        compiler_params=_params(("arbitrary", "arbitrary")),
        name="retention",
    )(qa, ka, va, ga, gain, bd)


def _mask_queries(q_ref, qm_ref):
    q = q_ref[...]
    hq = _head_of_lane(q.shape[1])
    for h in range(N_HEADS):
        qm_ref[h] = jnp.where(hq == h, q, jnp.zeros_like(q))


def _combine_heads(acc_ref, scale_of_head):
    hv = _head_of_lane(GROUP_WIDTH)
    o = jnp.zeros(acc_ref.shape[1:], F32)
    for h in range(N_HEADS):
        o = o + jnp.where(hv == h, acc_ref[h] * scale_of_head(h), 0.0)
    return o


def _head_rms(o, bd, gain):
    return o * lax.rsqrt(_group_mean(o * o, bd) + EPS) * gain


def _softmax_attn_body(*refs, tq, has_bias):
    if has_bias:
        q_ref, k_ref, v_ref, bias_ref, gain_ref, bd_ref, o_ref, qm_ref, m_ref, l_ref, acc_ref = refs
    else:
        q_ref, k_ref, v_ref, gain_ref, bd_ref, o_ref, qm_ref, m_ref, l_ref, acc_ref = refs
        bias_ref = None
    qi = pl.program_id(1)
    _mask_queries(q_ref, qm_ref)
    m_ref[...] = jnp.full_like(m_ref, NEG_INF)
    l_ref[...] = jnp.zeros_like(l_ref)
    acc_ref[...] = jnp.zeros_like(acc_ref)
    row = lax.broadcasted_iota(jnp.int32, (tq, tq), 0)
    col = lax.broadcasted_iota(jnp.int32, (tq, tq), 1)

    def block(j, diagonal):
        off = pl.multiple_of(j * tq, tq)
        kb = k_ref[pl.ds(off, tq), :]
        vb = v_ref[pl.ds(off, tq), :]
        for h in range(N_HEADS):
            s = _dot_nt(qm_ref[h], kb)
            if has_bias:
                s = s - bias_ref[h:h + 1, pl.ds(off, tq)]
            if diagonal:
                s = jnp.where(col <= row, s, NEG_INF)
            m_prev = m_ref[h]
            m_new = jnp.maximum(m_prev, jnp.max(s, axis=1, keepdims=True))
            alpha = jnp.exp(m_prev - m_new)
            p = jnp.exp(s - m_new)
            l_ref[h] = alpha * l_ref[h] + jnp.sum(p, axis=1, keepdims=True)
            acc_ref[h] = alpha * acc_ref[h] + _dot(p.astype(BF16), vb)
            m_ref[h] = m_new

    def body(j, carry):
        block(j, False)
        return carry

    lax.fori_loop(0, qi, body, 0)
    block(qi, True)
    o = _combine_heads(acc_ref, lambda h: 1.0 / l_ref[h])
    o_ref[...] = _head_rms(o, bd_ref[...], gain_ref[...]).astype(BF16)


def _softmax_attn(q, k, v, bias_t, gain, bd, batch, seq, tq):
    t, wq = q.shape
    gw = GROUP_WIDTH
    nq = seq // tq
    has_bias = bias_t is not None
    in_specs = [pl.BlockSpec((tq, wq), lambda b, i: (b * nq + i, 0)),
                pl.BlockSpec((seq, wq), lambda b, i: (b, 0)),
                pl.BlockSpec((seq, gw), lambda b, i: (b, 0))]
    args = [q, k, v]
    if has_bias:
        in_specs.append(pl.BlockSpec((8, seq), lambda b, i: (0, b)))
        args.append(bias_t)
    in_specs += [_full(gain.shape), _full(bd.shape)]
    args += [gain, bd]
    return pl.pallas_call(
        functools.partial(_softmax_attn_body, tq=tq, has_bias=has_bias),
        grid=(batch, nq),
        in_specs=in_specs,
        out_specs=pl.BlockSpec((tq, gw), lambda b, i: (b * nq + i, 0)),
        out_shape=jax.ShapeDtypeStruct((t, gw), BF16),
        scratch_shapes=[pltpu.VMEM((N_HEADS, tq, wq), BF16),
                        pltpu.VMEM((N_HEADS, tq, 1), F32),
                        pltpu.VMEM((N_HEADS, tq, 1), F32),
                        pltpu.VMEM((N_HEADS, tq, gw), F32)],
        compiler_params=_params(("arbitrary", "arbitrary")),
        name="fox_attn" if has_bias else "mla_attn",
    )(*args)


def _stickbreak_body(q_ref, k_ref, v_ref, gain_ref, bd_ref, tri_ref, o_ref,
                     qm_ref, carry_ref, acc_ref, *, tq):
    qi = pl.program_id(1)
    _mask_queries(q_ref, qm_ref)
    carry_ref[...] = jnp.zeros_like(carry_ref)
    acc_ref[...] = jnp.zeros_like(acc_ref)
    row = lax.broadcasted_iota(jnp.int32, (tq, tq), 0)
    col = lax.broadcasted_iota(jnp.int32, (tq, tq), 1)

    def block(j, diagonal):
        off = pl.multiple_of(j * tq, tq)
        kb = k_ref[pl.ds(off, tq), :]
        vb = v_ref[pl.ds(off, tq), :]
        tri = tri_ref[...]
        for h in range(N_HEADS):
            z = _dot_nt(qm_ref[h], kb)
            log_beta = _log_sigmoid(z)
            log_rest = log_beta - z
            if diagonal:
                log_rest = jnp.where(col < row, log_rest, 0.0)
            after = _split_dot(log_rest, tri, 2) + carry_ref[h]
            w = jnp.exp(log_beta + after)
            if diagonal:
                w = jnp.where(col < row, w, 0.0)
            acc_ref[h] = acc_ref[h] + _dot(w.astype(BF16), vb)
            carry_ref[h] = carry_ref[h] + jnp.sum(log_rest, axis=1, keepdims=True)

    block(qi, True)

    def body(it, carry):
        block(qi - 1 - it, False)
        return carry

    lax.fori_loop(0, qi, body, 0)
    o = _combine_heads(acc_ref, lambda h: 1.0)
    o_ref[...] = _head_rms(o, bd_ref[...], gain_ref[...]).astype(BF16)


def _stickbreak(q, k, v, gain, bd, batch, seq, tq):
    t, gw = q.shape
    nq = seq // tq
    tri = (jnp.arange(tq)[:, None] > jnp.arange(tq)[None, :]).astype(BF16)
    return pl.pallas_call(
        functools.partial(_stickbreak_body, tq=tq),
        grid=(batch, nq),
        in_specs=[pl.BlockSpec((tq, gw), lambda b, i: (b * nq + i, 0)),
                  pl.BlockSpec((seq, gw), lambda b, i: (b, 0)),
                  pl.BlockSpec((seq, gw), lambda b, i: (b, 0)),
                  _full(gain.shape), _full(bd.shape), _full(tri.shape)],
        out_specs=pl.BlockSpec((tq, gw), lambda b, i: (b * nq + i, 0)),
        out_shape=jax.ShapeDtypeStruct((t, gw), BF16),
        scratch_shapes=[pltpu.VMEM((N_HEADS, tq, gw), BF16),
                        pltpu.VMEM((N_HEADS, tq, 1), F32),
                        pltpu.VMEM((N_HEADS, tq, gw), F32)],
        compiler_params=_params(("arbitrary", "arbitrary")),
        name="stickbreak_attn",
    )(q, k, v, gain, bd, tri)


def _outproj_body(x_ref, ya_ref, yb_ref, yc_ref, yd_ref, wo_ref, gf_ref, wq_ref, sk_ref,
                  x1_ref, h2_ref, sc_ref):
    gw = GROUP_WIDTH
    acc = x_ref[...]
    for n, y_ref in enumerate((ya_ref, yb_ref, yc_ref, yd_ref)):
        acc = acc + _dot(y_ref[...], wo_ref[n * gw:(n + 1) * gw, :])
    x1_ref[...] = acc
    h2 = _rms(acc, gf_ref[...]).astype(BF16)
    h2_ref[...] = h2
    q = _dot(h2, wq_ref[...])
    for hp in range(sc_ref.shape[0]):
        sc_ref[hp] = _dot_nt(sk_ref[hp], q[:, hp * LANES:(hp + 1) * LANES].astype(BF16))


def _outproj(x2, ys, wo, gf, wq, sk, tm):
    t, d = x2.shape
    gw = GROUP_WIDTH
    nk = sk.shape[0]
    tok = lambda w: pl.BlockSpec((tm, w), lambda i: (i, 0))
    return pl.pallas_call(
        _outproj_body,
        grid=(t // tm,),
        in_specs=[tok(d)] + [tok(gw)] * 4 + [_full(wo.shape), _full(gf.shape), _full(wq.shape), _full(sk.shape)],
        out_specs=[tok(d), tok(d), pl.BlockSpec((nk, PEER_KEYS, tm), lambda i: (0, 0, i))],
        out_shape=[jax.ShapeDtypeStruct((t, d), F32), jax.ShapeDtypeStruct((t, d), BF16),
                   jax.ShapeDtypeStruct((nk, PEER_KEYS, t), F32)],
        compiler_params=_params(("arbitrary",)),
        name="outproj_peerq",
    )(x2, *ys, wo, gf, wq, sk)


def _top_k_rows(s, k, big):
    rows = lax.broadcasted_iota(jnp.int32, s.shape, 0).astype(F32)
    rank = jnp.full(s.shape, big, F32)
    vals = []
    for r in range(k):
        m = jnp.max(s, axis=0, keepdims=True)
        idx = jnp.min(jnp.where(s == m, rows, float(s.shape[0])), axis=0, keepdims=True)
        hit = rows == idx
        s = jnp.where(hit, NEG_INF, s)
        rank = jnp.where(hit, float(r), rank)
        vals.append(m)
    return vals, rank


def _route_body(sc_ref, rank2_ref, cnt_ref, e1_ref, e2_ref, cand_ref):
    k = PEER_TOPK
    not_selected = float(2 * k)

    def head(h, carry):
        s1 = sc_ref[2 * h]
        s2 = sc_ref[2 * h + 1]
        v1, rank1 = _top_k_rows(s1, k, not_selected)
        v2, rank2 = _top_k_rows(s2, k, not_selected)
        cand_ref[...] = jnp.full(cand_ref.shape, NEG_INF, F32)
        for p, (a, b) in enumerate(PEER_PAIRS):
            cand_ref[p:p + 1, :] = v1[a] + v2[b]
        cand = cand_ref[...]
        _, crank = _top_k_rows(cand, k, not_selected)
        sel = crank < float(k)
        top = v1[0] + v2[0]
        z = jnp.sum(jnp.where(sel, jnp.exp(cand - top), 0.0), axis=0, keepdims=True)
        prow = lax.broadcasted_iota(jnp.int32, cand.shape, 0)
        cnt = jnp.zeros(s1.shape, F32)
        row0 = 0
        for a in range(k):
            na = k // (a + 1)
            in_a = (prow >= row0) & (prow < row0 + na) & sel
            n_a = jnp.sum(jnp.where(in_a, 1.0, 0.0), axis=0, keepdims=True)
            cnt = cnt + jnp.where(rank1 == float(a), n_a, 0.0)
            row0 += na
        rank2_ref[h] = rank2.astype(BF16)
        cnt_ref[h] = cnt
        e1_ref[h] = jnp.exp(s1 - v1[0]) / z
        e2_ref[h] = jnp.exp(s2 - v2[0]).astype(BF16)
        return carry

    lax.fori_loop(0, PEER_HEADS, head, 0)


def _route(scores_t, tr):
    nk, nkeys, t = scores_t.shape
    out = lambda dt: jax.ShapeDtypeStruct((PEER_HEADS, nkeys, t), dt)
    spec = pl.BlockSpec((PEER_HEADS, nkeys, tr), lambda i: (0, 0, i))
    return pl.pallas_call(
        _route_body,
        grid=(t // tr,),
        in_specs=[pl.BlockSpec((nk, nkeys, tr), lambda i: (0, 0, i))],
        out_specs=[spec] * 4,
        out_shape=[out(BF16), out(F32), out(F32), out(BF16)],
        scratch_shapes=[pltpu.VMEM((PEER_PAIR_ROWS, tr), F32)],
        compiler_params=_params(("arbitrary",)),
        name="peer_route",
    )(scores_t)


def _peer_body(h_ref, u_ref, vt_ref, rank2_ref, cnt_ref, e1_ref, e2_ref, x1_ref, gfin_ref, o_ref,
               acc_ref, s0_ref, s1_ref, w_ref, *, groups, final_norm):
    n = pl.program_id(1)
    n_slabs = pl.num_programs(1) - 1
    nk = PEER_KEYS
    tm = h_ref.shape[0]
    s_ref = (s0_ref, s1_ref)

    @pl.when(n == 0)
    def _():
        acc_ref[...] = jnp.zeros_like(acc_ref)
        s1_ref[...] = jnp.zeros_like(s1_ref)

    def step(p):
        q = 1 - p
        s_ref[p][...] = _dot_nt(u_ref[...], h_ref[...])
        prev = jnp.clip(n - 1, 0, n_slabs - 1)
        for lanes in (slice(0, tm // 2), slice(tm // 2, tm)):
            for g in range(groups):
                i = prev * groups + g
                s = s_ref[q][g * nk:(g + 1) * nk, lanes]
                act = 0.5 * s * (1.0 + lax.erf(s * (2.0 ** -0.5)))
                gate = None
                for h in range(PEER_HEADS):
                    chosen = rank2_ref[h, :, lanes] < cnt_ref[h, pl.ds(i, 1), lanes].astype(BF16)
                    term = (jnp.where(chosen, e2_ref[h, :, lanes], jnp.zeros((), BF16))
                            * e1_ref[h, pl.ds(i, 1), lanes].astype(BF16))
                    gate = term if gate is None else gate + term
                w_ref[g * nk:(g + 1) * nk, lanes] = gate * act.astype(BF16)
            acc_ref[:, lanes] += _dot(vt_ref[...], w_ref[:, lanes])

    for p in range(2):
        pl.when(n % 2 == p)(functools.partial(step, p))

    @pl.when(n == pl.num_programs(1) - 1)
    def _():
        y = x1_ref[...] + acc_ref[...].T
        if final_norm:
            y = _rms(y, gfin_ref[...])
        o_ref[...] = y


def _peer(h2, u, vt, rank2, cnt, e1, e2, x1, gfin, final_norm, tm, groups):
    t, d = h2.shape
    ne = u.shape[0]
    slab = groups * PEER_KEYS
    n_slabs = ne // slab
    tok = pl.BlockSpec((tm, d), lambda i, s: (i, 0))
    route = pl.BlockSpec((PEER_HEADS, PEER_KEYS, tm), lambda i, s: (0, 0, i))
    return pl.pallas_call(
        functools.partial(_peer_body, groups=groups, final_norm=final_norm),
        grid=(t // tm, n_slabs + 1),
        in_specs=[tok, pl.BlockSpec((slab, d), lambda i, s: (jnp.minimum(s, n_slabs - 1), 0)),
                  pl.BlockSpec((d, slab), lambda i, s: (0, jnp.maximum(s - 1, 0))),
                  route, route, route, route, tok, _full(gfin.shape)],
        out_specs=tok,
        out_shape=jax.ShapeDtypeStruct((t, d), F32),
        scratch_shapes=[pltpu.VMEM((d, tm), F32), pltpu.VMEM((slab, tm), F32), pltpu.VMEM((slab, tm), F32),
                        pltpu.VMEM((slab, tm), BF16)],
        compiler_params=_params(("arbitrary", "arbitrary")),
        name="peer_experts",
    )(h2, u, vt, rank2, cnt, e1, e2, x1, gfin)


def _rope_tables(seq, dim, reps):
    pos = jnp.arange(seq, dtype=F32)
    inv = ROPE_BASE ** (-jnp.arange(0, dim, 2, dtype=F32) / dim)
    ang = pos[:, None] * inv[None, :]
    cos, sin = jnp.cos(ang), jnp.sin(ang)
    return (jnp.tile(jnp.concatenate([cos, cos], axis=1), (1, reps)),
            jnp.tile(jnp.concatenate([-sin, sin], axis=1), (1, reps)))


def _tile_sizes(seq, tokens):
    pick = lambda n, pref: pref if n % pref == 0 else n
    return dict(tm=pick(seq, 512), chunk=pick(seq, 256), tq=pick(seq, 256),
                tr=pick(tokens, 128), tp=pick(tokens, 512), groups=4)


def kernel(x, norm_mix, w_in, b_forget, norm_cq, norm_ckv, w_uq, w_ukv, head_gain, w_out, norm_ffn, w_query, sub_keys, expert_u, expert_v, norm_final):
    batch, seq, d = x.shape
    depth = w_in.shape[0]
    t = batch * seq
    gw = GROUP_WIDTH
    ts = _tile_sizes(seq, t)
    c64, s64 = _rope_tables(seq, HEAD_DIM, N_HEADS)
    c32, s32 = _rope_tables(seq, MLA_ROPE, N_HEADS)
    blk = jnp.arange(gw) // HEAD_DIM
    bd = jnp.where(blk[:, None] == blk[None, :], 1.0 / HEAD_DIM, 0.0).astype(BF16)
    row = lambda v: v.reshape(1, -1).astype(F32)

    x2 = x.reshape(t, d)
    for l in range(depth):
        w = w_in[l]
        o_ff = 7 * gw
        o_s = o_ff + N_HEADS
        o_cq = o_s + 3 * gw
        o_ckv = o_cq + w_uq.shape[1]
        o_kr = o_ckv + w_ukv.shape[1]
        w_main = jnp.concatenate(
            [w[:, :o_ff], w[:, o_s:o_kr], jnp.tile(w[:, o_kr:o_kr + MLA_ROPE], (1, N_HEADS))], axis=1).astype(BF16)
        wff_t = jnp.zeros((8, d), F32).at[:N_HEADS].set(w[:, o_ff:o_s].T).astype(BF16)
        bf = jnp.zeros((8, 1), F32).at[:N_HEADS, 0].set(b_forget[l])
        uq = w_uq[l].reshape(-1, N_HEADS, MLA_NOPE + MLA_ROPE)
        wuq = jnp.concatenate([uq[:, :, :MLA_NOPE].reshape(-1, gw),
                               uq[:, :, MLA_NOPE:].reshape(-1, N_HEADS * MLA_ROPE)], axis=1).astype(BF16)
        ukv = w_ukv[l].reshape(-1, N_HEADS, MLA_NOPE + HEAD_DIM)
        wukv = jnp.concatenate([ukv[:, :, :MLA_NOPE].reshape(-1, gw),
                                ukv[:, :, MLA_NOPE:].reshape(-1, gw)], axis=1).astype(BF16)
        (qa, ka, va, ga, fq, fk, fv, sq, sk, sv, qd, kd, vd, fcum_t) = _inproj(
            x2, row(norm_mix[l]), w_main, wff_t, bf, row(norm_cq[l]), row(norm_ckv[l]), wuq, wukv,
            c64, s64, c32, s32, seq, ts["tm"])
        gain = head_gain[l].astype(F32)
        ya = _retention(qa, ka, va, ga, row(gain[0:gw]), bd, batch, seq, ts["chunk"])
        yb = _softmax_attn(fq, fk, fv, fcum_t, row(gain[gw:2 * gw]), bd, batch, seq, ts["tq"])
        yc = _stickbreak(sq, sk, sv, row(gain[2 * gw:3 * gw]), bd, batch, seq, ts["tq"])
        yd = _softmax_attn(qd, kd, vd, None, row(gain[3 * gw:4 * gw]), bd, batch, seq, ts["tq"])
        keys = sub_keys[l].reshape(2 * PEER_HEADS, PEER_KEYS, -1).astype(BF16)
        x1, h2, scores_t = _outproj(x2, (ya, yb, yc, yd), w_out[l].astype(BF16), row(norm_ffn[l]),
                                    w_query[l].astype(BF16), keys, ts["tm"])
        rank2, cnt, e1, e2 = _route(scores_t, ts["tr"])
        x2 = _peer(h2, expert_u[l].astype(BF16), expert_v[l].T.astype(BF16), rank2, cnt, e1, e2, x1,
                   row(norm_final), l == depth - 1, ts["tp"], ts["groups"])
    return x2.reshape(batch, seq, d)
```

```python
import functools
import math

import numpy as np
import jax
import jax.numpy as jnp
from jax import lax
from jax.experimental import pallas as pl
from jax.experimental.pallas import tpu as pltpu

F32 = jnp.float32
BF16 = jnp.bfloat16

N_HEADS = 4
HEAD_DIM = 64
GROUP_WIDTH = N_HEADS * HEAD_DIM
MLA_NOPE = 64
MLA_ROPE = 32
ROPE_BASE = 10000.0
PEER_HEADS = 8
PEER_KEYS = 128
PEER_TOPK = 16
EPS = 1e-6
NEG_INF = float("-inf")

LANES = 128
VMEM_LIMIT = 56 * 1024 * 1024

PEER_PAIRS = tuple((a, b) for a in range(PEER_TOPK) for b in range(PEER_TOPK // (a + 1)))
PEER_PAIR_ROWS = -(-len(PEER_PAIRS) // 8) * 8

_NT = (((1,), (1,)), ((), ()))
_TN = (((0,), (0,)), ((), ()))


def _dot(a, b):
    return jnp.dot(a, b, preferred_element_type=F32)


def _dot_nt(a, b):
    return lax.dot_general(a, b, _NT, preferred_element_type=F32)


def _dot_tn(a, b):
    return lax.dot_general(a, b, _TN, preferred_element_type=F32)


def _split_dot(x, m, terms):
    acc = None
    r = x
    for t in range(terms):
        p = r.astype(BF16)
        d = _dot(p, m)
        acc = d if acc is None else acc + d
        if t + 1 < terms:
            r = r - p.astype(F32)
    return acc


def _log_sigmoid(x):
    return jnp.minimum(x, 0.0) - jnp.log1p(jnp.exp(-jnp.abs(x)))


def _rms(x, g):
    return x * lax.rsqrt(jnp.mean(x * x, axis=-1, keepdims=True) + EPS) * g


def _rope(x, c, s_signed, half):
    outs = []
    for o in range(0, x.shape[1], LANES):
        xs = x[:, o:o + LANES]
        lane = lax.broadcasted_iota(jnp.int32, xs.shape, 1)
        first = (lane & (2 * half - 1)) < half
        rot = jnp.where(first, pltpu.roll(xs, LANES - half, 1), pltpu.roll(xs, half, 1))
        outs.append(xs * c[:, o:o + LANES] + rot * s_signed[:, o:o + LANES])
    return outs[0] if len(outs) == 1 else jnp.concatenate(outs, axis=1)


def _head_of_lane(width):
    lane = lax.broadcasted_iota(jnp.int32, (1, width), 1)
    if width == GROUP_WIDTH:
        return lane >> 6
    return jnp.where(lane < GROUP_WIDTH, lane >> 6, (lane - GROUP_WIDTH) >> 5)


def _params(semantics, flags=None):
    return pltpu.CompilerParams(dimension_semantics=semantics, vmem_limit_bytes=VMEM_LIMIT, flags=flags)


def _full(shape):
    nd = len(shape)
    return pl.BlockSpec(shape, lambda *_: (0,) * nd)


def _inproj_body(x_ref, g_ref, w_ref, wt_ref, bf_ref, ncq_ref, nckv_ref, wuqt_ref, wukv_ref, wukvt_ref,
                 c64_ref, s64_ref, c32_ref, s32_ref, c32t_ref, s32t_ref, tril_ref, rep_ref,
                 qa_ref, ka_ref, va_ref, ga_ref, fk_ref, sk_ref, kd_ref,
                 fqt_ref, fvt_ref, sqt_ref, svt_ref, qdt_ref, vdt_ref, fb_ref, carry_ref, *, tiles_per_seq):
    i = pl.program_id(0)
    hb = _rms(x_ref[...], g_ref[...]).astype(BF16)
    gw = GROUP_WIDTH

    def proj(k, n=gw):
        return _dot(hb, w_ref[:, k * gw:k * gw + n])

    c64, s64 = c64_ref[...], s64_ref[...]
    c32, s32 = c32_ref[...], s32_ref[...]
    scale = HEAD_DIM ** -0.5
    qa_ref[...] = _rope(proj(0), c64, s64, HEAD_DIM // 2).astype(BF16)
    ka_ref[...] = (_rope(proj(1), c64, s64, HEAD_DIM // 2) * scale).astype(BF16)
    va_ref[...] = proj(2).astype(BF16)
    ga_ref[...] = proj(3).astype(BF16)
    fk_ref[...] = proj(4).astype(BF16)
    sk_ref[...] = proj(5).astype(BF16)

    tt = _dot_nt(wt_ref[...], hb)
    fqt_ref[...] = (tt[0:gw] * scale).astype(BF16)
    fvt_ref[...] = tt[gw:2 * gw].astype(BF16)
    sqt_ref[...] = (tt[2 * gw:3 * gw] * scale).astype(BF16)
    svt_ref[...] = tt[3 * gw:4 * gw].astype(BF16)

    mla_scale = (MLA_NOPE + MLA_ROPE) ** -0.5
    cq = _rms(proj(6), ncq_ref[...]).astype(BF16)
    qt = _dot_nt(wuqt_ref[...], cq) * mla_scale
    qdt_ref[0:gw, :] = qt[0:gw].astype(BF16)
    half = MLA_ROPE // 2
    ct, st = c32t_ref[...], s32t_ref[...]
    for h in range(N_HEADS):
        r0 = gw + h * MLA_ROPE
        t1, t2 = qt[r0:r0 + half], qt[r0 + half:r0 + 2 * half]
        qdt_ref[r0:r0 + half, :] = (t1 * ct - t2 * st).astype(BF16)
        qdt_ref[r0 + half:r0 + 2 * half, :] = (t1 * st + t2 * ct).astype(BF16)
    ckv = _rms(proj(7, LANES), nckv_ref[...]).astype(BF16)
    kd_ref[:, 0:gw] = _dot(ckv, wukv_ref[...]).astype(BF16)
    vdt_ref[...] = _dot_nt(wukvt_ref[...], ckv).astype(BF16)
    kr = _dot(hb, w_ref[:, 7 * gw + LANES:7 * gw + 2 * LANES])
    kd_ref[:, gw:gw + LANES] = _rope(kr, c32, s32, MLA_ROPE // 2).astype(BF16)

    lf = _log_sigmoid(proj(8, LANES) + bf_ref[...])
    tril = tril_ref[...]
    cum = None
    r = lf
    for term in range(3):
        part = r.astype(BF16)
        d = _dot(tril, part)
        cum = d if cum is None else cum + d
        if term < 2:
            r = r - part.astype(F32)

    @pl.when(i % tiles_per_seq == 0)
    def _():
        carry_ref[...] = jnp.zeros_like(carry_ref)

    fc = cum + carry_ref[0:1, :]
    tm = fc.shape[0]
    carry_ref[...] = jnp.broadcast_to(fc[tm - 1:tm, :], carry_ref.shape)
    fb_ref[...] = _split_dot(fc, rep_ref[...], 3)


def _inproj(x2, g, w_main, w_t, bf, ncq, nckv, wuq_t, wukv_n, wukv_vt, c64, s64, c32, s32, c32t, s32t, seq, tm):
    t, d = x2.shape
    gw = GROUP_WIDTH
    tiles_per_seq = seq // tm
    tril = (jnp.arange(tm)[:, None] >= jnp.arange(tm)[None, :]).astype(BF16)
    lane_head = jnp.arange(N_HEADS * LANES) // LANES
    rep = (jnp.arange(LANES)[:, None] == lane_head[None, :]).astype(BF16)
    tok = lambda w: pl.BlockSpec((tm, w), lambda i: (i, 0))
    chan = lambda c: pl.BlockSpec((c, tm), lambda i: (0, i))
    pos = lambda w: pl.BlockSpec((tm, w), lambda i: (i % tiles_per_seq, 0))
    pos_t = pl.BlockSpec((MLA_ROPE // 2, tm), lambda i: (0, i % tiles_per_seq))
    tok_widths = [gw] * 6 + [gw + LANES]
    chan_rows = [gw] * 4 + [gw + LANES, gw]
    out_shape = ([jax.ShapeDtypeStruct((t, w), BF16) for w in tok_widths]
                 + [jax.ShapeDtypeStruct((c, t), BF16) for c in chan_rows]
                 + [jax.ShapeDtypeStruct((t, N_HEADS * LANES), F32)])
    out_specs = [tok(w) for w in tok_widths] + [chan(c) for c in chan_rows] + [tok(N_HEADS * LANES)]
    return pl.pallas_call(
        functools.partial(_inproj_body, tiles_per_seq=tiles_per_seq),
        grid=(t // tm,),
        in_specs=[tok(d), _full(g.shape), _full(w_main.shape), _full(w_t.shape), _full(bf.shape),
                  _full(ncq.shape), _full(nckv.shape), _full(wuq_t.shape),
                  _full(wukv_n.shape), _full(wukv_vt.shape),
                  pos(gw), pos(gw), pos(LANES), pos(LANES), pos_t, pos_t, _full(tril.shape), _full(rep.shape)],
        out_specs=out_specs,
        out_shape=out_shape,
        scratch_shapes=[pltpu.VMEM((8, LANES), F32)],
        compiler_params=_params(("arbitrary",)),
        name="inproj",
    )(x2, g, w_main, w_t, bf, ncq, nckv, wuq_t, wukv_n, wukv_vt, c64, s64, c32, s32, c32t, s32t, tril, rep)


def _group_mean(x, bd):
    return _split_dot(x, bd, 2)


def _retention_body(q_ref, k_ref, v_ref, g_ref, gain_ref, bd_ref, o_ref, state_ref, *, chunk):
    ci = pl.program_id(1)

    @pl.when(ci == 0)
    def _():
        state_ref[...] = jnp.zeros_like(state_ref)

    gw = GROUP_WIDTH
    log_gamma = [math.log1p(-(2.0 ** (-5.0 - h))) for h in range(N_HEADS)]
    hl = _head_of_lane(gw)
    lg = jnp.zeros((1, gw), F32)
    for h in range(N_HEADS):
        lg = jnp.where(hl == h, log_gamma[h], lg)

    q, k, v = q_ref[...], k_ref[...], v_ref[...]
    ri = lax.broadcasted_iota(jnp.int32, (chunk, chunk), 0)
    cj = lax.broadcasted_iota(jnp.int32, (chunk, chunk), 1)
    rel = jnp.maximum(ri - cj, 0).astype(F32)
    causal = ri >= cj
    o = jnp.zeros((chunk, gw), F32)
    for h in range(N_HEADS):
        qm = jnp.where(hl == h, q, jnp.zeros_like(q))
        inner = _dot_nt(qm, k) * jnp.where(causal, jnp.exp(log_gamma[h] * rel), 0.0)
        o = o + jnp.where(hl == h, _dot(inner.astype(BF16), v), 0.0)

    pos = lax.broadcasted_iota(jnp.int32, (chunk, 1), 0).astype(F32)
    st = state_ref[...]
    q_dec = (q.astype(F32) * jnp.exp(lg * (pos + 1.0))).astype(BF16)
    o = o + _dot(q_dec, st.astype(BF16))
    k_dec = (k.astype(F32) * jnp.exp(lg * (chunk - 1.0 - pos))).astype(BF16)
    r2 = lax.broadcasted_iota(jnp.int32, (gw, gw), 0) >> 6
    c2 = lax.broadcasted_iota(jnp.int32, (gw, gw), 1) >> 6
    state_ref[...] = jnp.where(r2 == c2, st * jnp.exp(lg * float(chunk)) + _dot_tn(k_dec, v), 0.0)

    bd = bd_ref[...]
    oc = o - _group_mean(o, bd)
    y = oc * lax.rsqrt(_group_mean(oc * oc, bd) + EPS) * gain_ref[...]
    g = g_ref[...].astype(F32)
    o_ref[...] = (y * (g / (1.0 + jnp.exp(-g)))).astype(BF16)


def _retention(qa, ka, va, ga, gain, bd, batch, seq, chunk):
    t, gw = qa.shape
    n = seq // chunk
    blk = pl.BlockSpec((chunk, gw), lambda b, c: (b * n + c, 0))
    return pl.pallas_call(
        functools.partial(_retention_body, chunk=chunk),
        grid=(batch, n),
        in_specs=[blk, blk, blk, blk, _full(gain.shape), _full(bd.shape)],
        out_specs=blk,
        out_shape=jax.ShapeDtypeStruct((t, gw), BF16),
        scratch_shapes=[pltpu.VMEM((gw, gw), F32)],
        compiler_params=_params(("arbitrary", "arbitrary")),
        name="retention",
    )(qa, ka, va, ga, gain, bd)


def _head_of_row(width):
    r = lax.broadcasted_iota(jnp.int32, (width, 1), 0)
    if width == GROUP_WIDTH:
        return r >> 6
    return jnp.where(r < GROUP_WIDTH, r >> 6, (r - GROUP_WIDTH) >> 5)


def _mask_queries(qt_ref, qm_ref):
    qt = qt_ref[...]
    hr = _head_of_row(qt.shape[0])
    for h in range(N_HEADS):
        qm_ref[h] = jnp.where(hr == h, qt, jnp.zeros_like(qt))


def _finish_heads(acc_ref, scale_of_head, bd, gain):
    ot = jnp.concatenate([acc_ref[h] * scale_of_head(h) for h in range(N_HEADS)], axis=0)
    o = ot.T
    return o * lax.rsqrt(_group_mean(o * o, bd) + EPS) * gain


def _softmax_attn_body(*refs, tq, has_bias):
    if has_bias:
        qt_ref, k_ref, vt_ref, bias_ref, gain_ref, bd_ref, o_ref, qm_ref, m_ref, l_ref, acc_ref = refs
    else:
        qt_ref, k_ref, vt_ref, gain_ref, bd_ref, o_ref, qm_ref, m_ref, l_ref, acc_ref = refs
        bias_ref = None
    qi = pl.program_id(1)
    _mask_queries(qt_ref, qm_ref)
    m_ref[...] = jnp.full_like(m_ref, NEG_INF)
    l_ref[...] = jnp.zeros_like(l_ref)
    acc_ref[...] = jnp.zeros_like(acc_ref)
    key = lax.broadcasted_iota(jnp.int32, (tq, tq), 0)
    qry = lax.broadcasted_iota(jnp.int32, (tq, tq), 1)

    def block(j, diagonal):
        off = pl.multiple_of(j * tq, tq)
        kb = k_ref[pl.ds(off, tq), :]
        scores = [_dot(kb, qm_ref[h]) for h in range(N_HEADS)]
        for h in range(N_HEADS):
            s = scores[h]
            if has_bias:
                kbias = bias_ref[pl.ds(off, tq), h * LANES:(h + 1) * LANES]
                s = s - jnp.concatenate([kbias] * (tq // LANES), axis=1)
            if diagonal:
                s = jnp.where(key <= qry, s, NEG_INF)
            m_prev = m_ref[h]
            m_new = jnp.maximum(m_prev, jnp.max(s, axis=0, keepdims=True))
            alpha = jnp.exp(m_prev - m_new)
            p = jnp.exp(s - m_new)
            l_ref[h] = alpha * l_ref[h] + jnp.sum(p, axis=0, keepdims=True)
            vt = vt_ref[h * HEAD_DIM:(h + 1) * HEAD_DIM, pl.ds(off, tq)]
            acc_ref[h] = alpha * acc_ref[h] + _dot(vt, p.astype(BF16))
            m_ref[h] = m_new

    def body(j, carry):
        block(j, False)
        return carry

    lax.fori_loop(0, qi, body, 0)
    block(qi, True)
    y = _finish_heads(acc_ref, lambda h: 1.0 / l_ref[h], bd_ref[...], gain_ref[...])
    o_ref[...] = y.astype(BF16)


def _softmax_attn(qt, k, vt, bias, gain, bd, batch, seq, tq):
    wq, t = qt.shape
    gw = GROUP_WIDTH
    nq = seq // tq
    has_bias = bias is not None
    in_specs = [pl.BlockSpec((wq, tq), lambda b, i: (0, b * nq + i)),
                pl.BlockSpec((seq, wq), lambda b, i: (b, 0)),
                pl.BlockSpec((gw, seq), lambda b, i: (0, b))]
    args = [qt, k, vt]
    if has_bias:
        in_specs.append(pl.BlockSpec((seq, bias.shape[1]), lambda b, i: (b, 0)))
        args.append(bias)
    in_specs += [_full(gain.shape), _full(bd.shape)]
    args += [gain, bd]
    return pl.pallas_call(
        functools.partial(_softmax_attn_body, tq=tq, has_bias=has_bias),
        grid=(batch, nq),
        in_specs=in_specs,
        out_specs=pl.BlockSpec((tq, gw), lambda b, i: (b * nq + i, 0)),
        out_shape=jax.ShapeDtypeStruct((t, gw), BF16),
        scratch_shapes=[pltpu.VMEM((N_HEADS, wq, tq), BF16),
                        pltpu.VMEM((N_HEADS, 1, tq), F32),
                        pltpu.VMEM((N_HEADS, 1, tq), F32),
                        pltpu.VMEM((N_HEADS, HEAD_DIM, tq), F32)],
        compiler_params=_params(("arbitrary", "arbitrary")),
        name="fox_attn" if has_bias else "mla_attn",
    )(*args)


def _stickbreak_body(qt_ref, k_ref, vt_ref, gain_ref, bd_ref, tri_ref, o_ref,
                     qm_ref, carry_ref, acc_ref, *, tq):
    qi = pl.program_id(1)
    _mask_queries(qt_ref, qm_ref)
    carry_ref[...] = jnp.zeros_like(carry_ref)
    acc_ref[...] = jnp.zeros_like(acc_ref)
    key = lax.broadcasted_iota(jnp.int32, (tq, tq), 0)
    qry = lax.broadcasted_iota(jnp.int32, (tq, tq), 1)

    def block(j, diagonal):
        off = pl.multiple_of(j * tq, tq)
        kb = k_ref[pl.ds(off, tq), :]
        tri = tri_ref[...]
        zs = [_dot(kb, qm_ref[h]) for h in range(N_HEADS)]
        log_betas, afters = [], []
        for h in range(N_HEADS):
            log_beta = _log_sigmoid(zs[h])
            log_rest = log_beta - zs[h]
            if diagonal:
                log_rest = jnp.where(key < qry, log_rest, 0.0)
            hi = log_rest.astype(BF16)
            lo = (log_rest - hi.astype(F32)).astype(BF16)
            afters.append(_dot(tri, hi) + _dot(tri, lo) + carry_ref[h])
            log_betas.append(log_beta)
            carry_ref[h] = carry_ref[h] + jnp.sum(log_rest, axis=0, keepdims=True)
        for h in range(N_HEADS):
            w = jnp.exp(log_betas[h] + afters[h])
            if diagonal:
                w = jnp.where(key < qry, w, 0.0)
            vt = vt_ref[h * HEAD_DIM:(h + 1) * HEAD_DIM, pl.ds(off, tq)]
            acc_ref[h] = acc_ref[h] + _dot(vt, w.astype(BF16))

    block(qi, True)

    def body(it, carry):
        block(qi - 1 - it, False)
        return carry

    lax.fori_loop(0, qi, body, 0)
    y = _finish_heads(acc_ref, lambda h: 1.0, bd_ref[...], gain_ref[...])
    o_ref[...] = y.astype(BF16)


def _stickbreak(qt, k, vt, gain, bd, batch, seq, tq):
    gw, t = qt.shape
    nq = seq // tq
    tri = (jnp.arange(tq)[None, :] > jnp.arange(tq)[:, None]).astype(BF16)
    return pl.pallas_call(
        functools.partial(_stickbreak_body, tq=tq),
        grid=(batch, nq),
        in_specs=[pl.BlockSpec((gw, tq), lambda b, i: (0, b * nq + i)),
                  pl.BlockSpec((seq, gw), lambda b, i: (b, 0)),
                  pl.BlockSpec((gw, seq), lambda b, i: (0, b)),
                  _full(gain.shape), _full(bd.shape), _full(tri.shape)],
        out_specs=pl.BlockSpec((tq, gw), lambda b, i: (b * nq + i, 0)),
        out_shape=jax.ShapeDtypeStruct((t, gw), BF16),
        scratch_shapes=[pltpu.VMEM((N_HEADS, gw, tq), BF16),
                        pltpu.VMEM((N_HEADS, 1, tq), F32),
                        pltpu.VMEM((N_HEADS, HEAD_DIM, tq), F32)],
        compiler_params=_params(("arbitrary", "arbitrary")),
        name="stickbreak_attn",
    )(qt, k, vt, gain, bd, tri)


def _outproj_body(x_ref, ya_ref, yb_ref, yc_ref, yd_ref, wo_ref, gf_ref, wq_ref, sk_ref,
                  x1_ref, h2_ref, sc_ref):
    gw = GROUP_WIDTH
    acc = x_ref[...]
    for n, y_ref in enumerate((ya_ref, yb_ref, yc_ref, yd_ref)):
        acc = acc + _dot(y_ref[...], wo_ref[n * gw:(n + 1) * gw, :])
    x1_ref[...] = acc
    h2 = _rms(acc, gf_ref[...]).astype(BF16)
    h2_ref[...] = h2
    q = _dot(h2, wq_ref[...])
    for hp in range(sc_ref.shape[0]):
        sc_ref[hp] = _dot_nt(sk_ref[hp], q[:, hp * LANES:(hp + 1) * LANES].astype(BF16))


def _outproj(x2, ys, wo, gf, wq, sk, tm):
    t, d = x2.shape
    gw = GROUP_WIDTH
    nk = sk.shape[0]
    tok = lambda w: pl.BlockSpec((tm, w), lambda i: (i, 0))
    return pl.pallas_call(
        _outproj_body,
        grid=(t // tm,),
        in_specs=[tok(d)] + [tok(gw)] * 4 + [_full(wo.shape), _full(gf.shape), _full(wq.shape), _full(sk.shape)],
        out_specs=[tok(d), tok(d), pl.BlockSpec((nk, PEER_KEYS, tm), lambda i: (0, 0, i))],
        out_shape=[jax.ShapeDtypeStruct((t, d), F32), jax.ShapeDtypeStruct((t, d), BF16),
                   jax.ShapeDtypeStruct((nk, PEER_KEYS, t), F32)],
        compiler_params=_params(("arbitrary",)),
        name="outproj_peerq",
    )(x2, *ys, wo, gf, wq, sk)


def _top_k_rows(s, k, big):
    rows = lax.broadcasted_iota(jnp.int32, s.shape, 0).astype(F32)
    rank = jnp.full(s.shape, big, F32)
    vals = []
    for r in range(k):
        m = jnp.max(s, axis=0, keepdims=True)
        idx = jnp.min(jnp.where(s == m, rows, float(s.shape[0])), axis=0, keepdims=True)
        hit = rows == idx
        s = jnp.where(hit, NEG_INF, s)
        rank = jnp.where(hit, float(r), rank)
        vals.append(m)
    return vals, rank


def _route_body(sc_ref, rank2_ref, cnt_ref, e1_ref, e2_ref, cand_ref):
    k = PEER_TOPK
    not_selected = float(2 * k)

    def head(h, carry):
        s1 = sc_ref[2 * h]
        s2 = sc_ref[2 * h + 1]
        v1, rank1 = _top_k_rows(s1, k, not_selected)
        v2, rank2 = _top_k_rows(s2, k, not_selected)
        cand_ref[...] = jnp.full(cand_ref.shape, NEG_INF, F32)
        for p, (a, b) in enumerate(PEER_PAIRS):
            cand_ref[p:p + 1, :] = v1[a] + v2[b]
        cand = cand_ref[...]
        _, crank = _top_k_rows(cand, k, not_selected)
        sel = crank < float(k)
        top = v1[0] + v2[0]
        z = jnp.sum(jnp.where(sel, jnp.exp(cand - top), 0.0), axis=0, keepdims=True)
        prow = lax.broadcasted_iota(jnp.int32, cand.shape, 0)
        cnt = jnp.zeros(s1.shape, F32)
        row0 = 0
        for a in range(k):
            na = k // (a + 1)
            in_a = (prow >= row0) & (prow < row0 + na) & sel
            n_a = jnp.sum(jnp.where(in_a, 1.0, 0.0), axis=0, keepdims=True)
            cnt = cnt + jnp.where(rank1 == float(a), n_a, 0.0)
            row0 += na
        rank2_ref[h] = rank2.astype(BF16)
        cnt_ref[h] = cnt
        e1_ref[h] = jnp.exp(s1 - v1[0]) / z
        e2_ref[h] = jnp.exp(s2 - v2[0]).astype(BF16)
        return carry

    lax.fori_loop(0, PEER_HEADS, head, 0)


def _route(scores_t, tr):
    nk, nkeys, t = scores_t.shape
    out = lambda dt: jax.ShapeDtypeStruct((PEER_HEADS, nkeys, t), dt)
    spec = pl.BlockSpec((PEER_HEADS, nkeys, tr), lambda i: (0, 0, i))
    return pl.pallas_call(
        _route_body,
        grid=(t // tr,),
        in_specs=[pl.BlockSpec((nk, nkeys, tr), lambda i: (0, 0, i))],
        out_specs=[spec] * 4,
        out_shape=[out(BF16), out(F32), out(F32), out(BF16)],
        scratch_shapes=[pltpu.VMEM((PEER_PAIR_ROWS, tr), F32)],
        compiler_params=_params(("arbitrary",)),
        name="peer_route",
    )(scores_t)


def _peer_body(h_ref, u_ref, vt_ref, rank2_ref, cnt_ref, e1_ref, e2_ref, x1_ref, gfin_ref, o_ref,
               acc_ref, s0_ref, s1_ref, w_ref, *, groups, final_norm):
    n = pl.program_id(1)
    n_slabs = pl.num_programs(1) - 1
    nk = PEER_KEYS
    tm = h_ref.shape[0]
    s_ref = (s0_ref, s1_ref)

    @pl.when(n == 0)
    def _():
        acc_ref[...] = jnp.zeros_like(acc_ref)
        s1_ref[...] = jnp.zeros_like(s1_ref)

    def step(p):
        q = 1 - p
        s_ref[p][...] = _dot_nt(u_ref[...], h_ref[...])
        prev = jnp.clip(n - 1, 0, n_slabs - 1)
        for lanes in (slice(0, tm // 2), slice(tm // 2, tm)):
            for g in range(groups):
                i = prev * groups + g
                s = s_ref[q][g * nk:(g + 1) * nk, lanes]
                act = 0.5 * s * (1.0 + lax.erf(s * (2.0 ** -0.5)))
                gate = None
                for h in range(PEER_HEADS):
                    chosen = rank2_ref[h, :, lanes] < cnt_ref[h, pl.ds(i, 1), lanes].astype(BF16)
                    term = (jnp.where(chosen, e2_ref[h, :, lanes], jnp.zeros((), BF16))
                            * e1_ref[h, pl.ds(i, 1), lanes].astype(BF16))
                    gate = term if gate is None else gate + term
                w_ref[g * nk:(g + 1) * nk, lanes] = gate * act.astype(BF16)
            acc_ref[:, lanes] += _dot(vt_ref[...], w_ref[:, lanes])

    for p in range(2):
        pl.when(n % 2 == p)(functools.partial(step, p))

    @pl.when(n == pl.num_programs(1) - 1)
    def _():
        y = x1_ref[...] + acc_ref[...].T
        if final_norm:
            y = _rms(y, gfin_ref[...])
        o_ref[...] = y


def _peer(h2, u, v, rank2, cnt, e1, e2, x1, gfin, final_norm, tm, groups):
    t, d = h2.shape
    ne = u.shape[0]
    slab = groups * PEER_KEYS
    n_slabs = ne // slab
    vt = v.reshape(n_slabs, slab, d).transpose(0, 2, 1)
    tok = pl.BlockSpec((tm, d), lambda i, s: (i, 0))
    route = pl.BlockSpec((PEER_HEADS, PEER_KEYS, tm), lambda i, s: (0, 0, i))
    return pl.pallas_call(
        functools.partial(_peer_body, groups=groups, final_norm=final_norm),
        grid=(t // tm, n_slabs + 1),
        in_specs=[tok, pl.BlockSpec((slab, d), lambda i, s: (jnp.minimum(s, n_slabs - 1), 0)),
                  pl.BlockSpec((None, d, slab), lambda i, s: (jnp.maximum(s - 1, 0), 0, 0)),
                  route, route, route, route, tok, _full(gfin.shape)],
        out_specs=tok,
        out_shape=jax.ShapeDtypeStruct((t, d), F32),
        scratch_shapes=[pltpu.VMEM((d, tm), F32), pltpu.VMEM((slab, tm), F32), pltpu.VMEM((slab, tm), F32),
                        pltpu.VMEM((slab, tm), BF16)],
        compiler_params=_params(("arbitrary", "arbitrary")),
        name="peer_experts",
    )(h2, u, vt, rank2, cnt, e1, e2, x1, gfin)


def _rope_tables(seq, dim, reps):
    pos = jnp.arange(seq, dtype=F32)
    inv = ROPE_BASE ** (-jnp.arange(0, dim, 2, dtype=F32) / dim)
    ang = pos[:, None] * inv[None, :]
    cos, sin = jnp.cos(ang), jnp.sin(ang)
    return (jnp.tile(jnp.concatenate([cos, cos], axis=1), (1, reps)),
            jnp.tile(jnp.concatenate([-sin, sin], axis=1), (1, reps)), cos.T, sin.T)


def _tile_sizes(seq, tokens):
    pick = lambda n, pref: pref if n % pref == 0 else n
    return dict(tm=pick(seq, 512), chunk=pick(seq, 256), tq=pick(seq, 256),
                tr=pick(tokens, 256), tp=pick(tokens, 512), groups=4)


def kernel(x, norm_mix, w_in, b_forget, norm_cq, norm_ckv, w_uq, w_ukv, head_gain, w_out, norm_ffn, w_query, sub_keys, expert_u, expert_v, norm_final):
    batch, seq, d = x.shape
    depth = w_in.shape[0]
    t = batch * seq
    gw = GROUP_WIDTH
    ts = _tile_sizes(seq, t)
    c64, s64, _, _ = _rope_tables(seq, HEAD_DIM, N_HEADS)
    c32, s32, c32t, s32t = _rope_tables(seq, MLA_ROPE, N_HEADS)
    blk = jnp.arange(gw) // HEAD_DIM
    bd = jnp.where(blk[:, None] == blk[None, :], 1.0 / HEAD_DIM, 0.0).astype(BF16)
    row = lambda v: v.reshape(1, -1).astype(F32)
    pad_lanes = lambda a: jnp.pad(a, ((0, 0), (0, LANES - a.shape[1])))

    x2 = x.reshape(t, d)
    for l in range(depth):
        w = w_in[l]
        cols = lambda k: w[:, k * gw:(k + 1) * gw]
        o_ff = 7 * gw
        o_s = o_ff + N_HEADS
        scols = lambda k: w[:, o_s + k * gw:o_s + (k + 1) * gw]
        o_cq = o_s + 3 * gw
        o_ckv = o_cq + w_uq.shape[1]
        o_kr = o_ckv + w_ukv.shape[1]
        w_main = jnp.concatenate(
            [w[:, :4 * gw], cols(5), scols(1), w[:, o_cq:o_kr], jnp.tile(w[:, o_kr:o_kr + MLA_ROPE], (1, N_HEADS)),
             pad_lanes(w[:, o_ff:o_s])], axis=1).astype(BF16)
        w_t = jnp.concatenate([cols(4), cols(6), scols(0), scols(2)], axis=1).T.astype(BF16)
        bf = pad_lanes(b_forget[l].reshape(1, -1).astype(F32))
        uq = w_uq[l].reshape(-1, N_HEADS, MLA_NOPE + MLA_ROPE)
        wuq_t = jnp.concatenate([uq[:, :, :MLA_NOPE].reshape(-1, gw),
                                 uq[:, :, MLA_NOPE:].reshape(-1, N_HEADS * MLA_ROPE)], axis=1).T.astype(BF16)
        ukv = w_ukv[l].reshape(-1, N_HEADS, MLA_NOPE + HEAD_DIM)
        wukv_n = ukv[:, :, :MLA_NOPE].reshape(-1, gw).astype(BF16)
        wukv_vt = ukv[:, :, MLA_NOPE:].reshape(-1, gw).T.astype(BF16)
        (qa, ka, va, ga, fk, sk, kd, fqt, fvt, sqt, svt, qdt, vdt, fbias) = _inproj(
            x2, row(norm_mix[l]), w_main, w_t, bf, row(norm_cq[l]), row(norm_ckv[l]), wuq_t, wukv_n, wukv_vt,
            c64, s64, c32, s32, c32t, s32t, seq, ts["tm"])
        gain = head_gain[l].astype(F32)
        ya = _retention(qa, ka, va, ga, row(gain[0:gw]), bd, batch, seq, ts["chunk"])
        yb = _softmax_attn(fqt, fk, fvt, fbias, row(gain[gw:2 * gw]), bd, batch, seq, ts["tq"])
        yc = _stickbreak(sqt, sk, svt, row(gain[2 * gw:3 * gw]), bd, batch, seq, ts["tq"])
        yd = _softmax_attn(qdt, kd, vdt, None, row(gain[3 * gw:4 * gw]), bd, batch, seq, ts["tq"])
        keys = sub_keys[l].reshape(2 * PEER_HEADS, PEER_KEYS, -1).astype(BF16)
        x1, h2, scores_t = _outproj(x2, (ya, yb, yc, yd), w_out[l].astype(BF16), row(norm_ffn[l]),
                                    w_query[l].astype(BF16), keys, ts["tm"])
        rank2, cnt, e1, e2 = _route(scores_t, ts["tr"])
        x2 = _peer(h2, expert_u[l].astype(BF16), expert_v[l].astype(BF16), rank2, cnt, e1, e2, x1,
                   row(norm_final), l == depth - 1, ts["tp"], ts["groups"])
    return x2.reshape(batch, seq, d)
```

```python
import functools
import math

import numpy as np
import jax
import jax.numpy as jnp
from jax import lax
from jax.experimental import pallas as pl
from jax.experimental.pallas import tpu as pltpu

F32 = jnp.float32
BF16 = jnp.bfloat16

N_HEADS = 4
HEAD_DIM = 64
GROUP_WIDTH = N_HEADS * HEAD_DIM
MLA_NOPE = 64
MLA_ROPE = 32
ROPE_BASE = 10000.0
PEER_HEADS = 8
PEER_KEYS = 128
PEER_TOPK = 16
EPS = 1e-6
NEG_INF = float("-inf")

LANES = 128
VMEM_LIMIT = 56 * 1024 * 1024

PEER_PAIRS = tuple((a, b) for a in range(PEER_TOPK) for b in range(PEER_TOPK // (a + 1)))
PEER_PAIR_ROWS = -(-len(PEER_PAIRS) // 8) * 8

_NT = (((1,), (1,)), ((), ()))
_TN = (((0,), (0,)), ((), ()))


def _dot(a, b):
    return jnp.dot(a, b, preferred_element_type=F32)


def _dot_nt(a, b):
    return lax.dot_general(a, b, _NT, preferred_element_type=F32)


def _dot_tn(a, b):
    return lax.dot_general(a, b, _TN, preferred_element_type=F32)


def _split_dot(x, m, terms):
    acc = None
    r = x
    for t in range(terms):
        p = r.astype(BF16)
        d = _dot(p, m)
        acc = d if acc is None else acc + d
        if t + 1 < terms:
            r = r - p.astype(F32)
    return acc


def _log_sigmoid(x):
    return jnp.minimum(x, 0.0) - jnp.log(1.0 + jnp.exp(-jnp.abs(x)))


def _rms(x, g):
    return x * lax.rsqrt(jnp.mean(x * x, axis=-1, keepdims=True) + EPS) * g


def _rope(x, c, s_signed, half):
    outs = []
    for o in range(0, x.shape[1], LANES):
        xs = x[:, o:o + LANES]
        lane = lax.broadcasted_iota(jnp.int32, xs.shape, 1)
        first = (lane & (2 * half - 1)) < half
        rot = jnp.where(first, pltpu.roll(xs, LANES - half, 1), pltpu.roll(xs, half, 1))
        outs.append(xs * c[:, o:o + LANES] + rot * s_signed[:, o:o + LANES])
    return outs[0] if len(outs) == 1 else jnp.concatenate(outs, axis=1)


def _head_of_lane(width):
    lane = lax.broadcasted_iota(jnp.int32, (1, width), 1)
    if width == GROUP_WIDTH:
        return lane >> 6
    return jnp.where(lane < GROUP_WIDTH, lane >> 6, (lane - GROUP_WIDTH) >> 5)


def _params(semantics, flags=None):
    return pltpu.CompilerParams(dimension_semantics=semantics, vmem_limit_bytes=VMEM_LIMIT, flags=flags)


def _full(shape):
    nd = len(shape)
    return pl.BlockSpec(shape, lambda *_: (0,) * nd)


def _inproj_body(x_ref, g_ref, w_ref, wt_ref, bf_ref, ncq_ref, nckv_ref, wuqt_ref, wukv_ref, wukvt_ref,
                 c64_ref, s64_ref, c32_ref, s32_ref, c32t_ref, s32t_ref, tril_ref, rep_ref,
                 qa_ref, ka_ref, va_ref, ga_ref, fk_ref, sk_ref, kd_ref,
                 fqt_ref, fvt_ref, sqt_ref, svt_ref, qdt_ref, vdt_ref, fb_ref, carry_ref, *, tiles_per_seq):
    i = pl.program_id(0)
    hb = _rms(x_ref[...], g_ref[...]).astype(BF16)
    gw = GROUP_WIDTH

    def proj(k, n=gw):
        return _dot(hb, w_ref[:, k * gw:k * gw + n])

    c64, s64 = c64_ref[...], s64_ref[...]
    c32, s32 = c32_ref[...], s32_ref[...]
    scale = HEAD_DIM ** -0.5
    qa_ref[...] = _rope(proj(0), c64, s64, HEAD_DIM // 2).astype(BF16)
    ka_ref[...] = (_rope(proj(1), c64, s64, HEAD_DIM // 2) * scale).astype(BF16)
    va_ref[...] = proj(2).astype(BF16)
    ga_ref[...] = proj(3).astype(BF16)
    fk_ref[...] = proj(4).astype(BF16)
    sk_ref[...] = proj(5).astype(BF16)

    tt = _dot_nt(wt_ref[...], hb)
    fqt_ref[...] = (tt[0:gw] * scale).astype(BF16)
    fvt_ref[...] = tt[gw:2 * gw].astype(BF16)
    sqt_ref[...] = (tt[2 * gw:3 * gw] * scale).astype(BF16)
    svt_ref[...] = tt[3 * gw:4 * gw].astype(BF16)

    mla_scale = (MLA_NOPE + MLA_ROPE) ** -0.5
    cq = _rms(proj(6), ncq_ref[...]).astype(BF16)
    qt = _dot_nt(wuqt_ref[...], cq) * mla_scale
    qdt_ref[0:gw, :] = qt[0:gw].astype(BF16)
    half = MLA_ROPE // 2
    ct, st = c32t_ref[...], s32t_ref[...]
    for h in range(N_HEADS):
        r0 = gw + h * MLA_ROPE
        t1, t2 = qt[r0:r0 + half], qt[r0 + half:r0 + 2 * half]
        qdt_ref[r0:r0 + half, :] = (t1 * ct - t2 * st).astype(BF16)
        qdt_ref[r0 + half:r0 + 2 * half, :] = (t1 * st + t2 * ct).astype(BF16)
    ckv = _rms(proj(7, LANES), nckv_ref[...]).astype(BF16)
    kd_ref[:, 0:gw] = _dot(ckv, wukv_ref[...]).astype(BF16)
    vdt_ref[...] = _dot_nt(wukvt_ref[...], ckv).astype(BF16)
    kr = _dot(hb, w_ref[:, 7 * gw + LANES:7 * gw + 2 * LANES])
    kd_ref[:, gw:gw + LANES] = _rope(kr, c32, s32, MLA_ROPE // 2).astype(BF16)

    lf = _log_sigmoid(proj(8, LANES) + bf_ref[...])
    tril = tril_ref[...]
    cum = None
    r = lf
    for term in range(3):
        part = r.astype(BF16)
        d = _dot(tril, part)
        cum = d if cum is None else cum + d
        if term < 2:
            r = r - part.astype(F32)

    @pl.when(i % tiles_per_seq == 0)
    def _():
        carry_ref[...] = jnp.zeros_like(carry_ref)

    fc = cum + carry_ref[0:1, :]
    tm = fc.shape[0]
    carry_ref[...] = jnp.broadcast_to(fc[tm - 1:tm, :], carry_ref.shape)
    fb_ref[...] = _split_dot(fc, rep_ref[...], 3)


def _inproj(x2, g, w_main, w_t, bf, ncq, nckv, wuq_t, wukv_n, wukv_vt, c64, s64, c32, s32, c32t, s32t, seq, tm):
    t, d = x2.shape
    gw = GROUP_WIDTH
    tiles_per_seq = seq // tm
    tril = (jnp.arange(tm)[:, None] >= jnp.arange(tm)[None, :]).astype(BF16)
    lane_head = jnp.arange(N_HEADS * LANES) // LANES
    rep = (jnp.arange(LANES)[:, None] == lane_head[None, :]).astype(BF16)
    tok = lambda w: pl.BlockSpec((tm, w), lambda i: (i, 0))
    chan = lambda c: pl.BlockSpec((c, tm), lambda i: (0, i))
    pos = lambda w: pl.BlockSpec((tm, w), lambda i: (i % tiles_per_seq, 0))
    pos_t = pl.BlockSpec((MLA_ROPE // 2, tm), lambda i: (0, i % tiles_per_seq))
    tok_widths = [gw] * 6 + [gw + LANES]
    chan_rows = [gw] * 4 + [gw + LANES, gw]
    out_shape = ([jax.ShapeDtypeStruct((t, w), BF16) for w in tok_widths]
                 + [jax.ShapeDtypeStruct((c, t), BF16) for c in chan_rows]
                 + [jax.ShapeDtypeStruct((t, N_HEADS * LANES), F32)])
    out_specs = [tok(w) for w in tok_widths] + [chan(c) for c in chan_rows] + [tok(N_HEADS * LANES)]
    return pl.pallas_call(
        functools.partial(_inproj_body, tiles_per_seq=tiles_per_seq),
        grid=(t // tm,),
        in_specs=[tok(d), _full(g.shape), _full(w_main.shape), _full(w_t.shape), _full(bf.shape),
                  _full(ncq.shape), _full(nckv.shape), _full(wuq_t.shape),
                  _full(wukv_n.shape), _full(wukv_vt.shape),
                  pos(gw), pos(gw), pos(LANES), pos(LANES), pos_t, pos_t, _full(tril.shape), _full(rep.shape)],
        out_specs=out_specs,
        out_shape=out_shape,
        scratch_shapes=[pltpu.VMEM((8, LANES), F32)],
        compiler_params=_params(("arbitrary",)),
        name="inproj",
    )(x2, g, w_main, w_t, bf, ncq, nckv, wuq_t, wukv_n, wukv_vt, c64, s64, c32, s32, c32t, s32t, tril, rep)


def _group_mean(x, bd):
    return _split_dot(x, bd, 2)


def _retention_body(q_ref, k_ref, v_ref, g_ref, gain_ref, bd_ref, o_ref, state_ref, *, chunk):
    ci = pl.program_id(1)

    @pl.when(ci == 0)
    def _():
        state_ref[...] = jnp.zeros_like(state_ref)

    gw = GROUP_WIDTH
    log_gamma = [math.log1p(-(2.0 ** (-5.0 - h))) for h in range(N_HEADS)]
    hl = _head_of_lane(gw)
    lg = jnp.zeros((1, gw), F32)
    for h in range(N_HEADS):
        lg = jnp.where(hl == h, log_gamma[h], lg)

    q, k, v = q_ref[...], k_ref[...], v_ref[...]
    ri = lax.broadcasted_iota(jnp.int32, (chunk, chunk), 0)
    cj = lax.broadcasted_iota(jnp.int32, (chunk, chunk), 1)
    rel = jnp.maximum(ri - cj, 0).astype(F32)
    causal = ri >= cj
    o = jnp.zeros((chunk, gw), F32)
    for h in range(N_HEADS):
        qm = jnp.where(hl == h, q, jnp.zeros_like(q))
        inner = _dot_nt(qm, k) * jnp.where(causal, jnp.exp(log_gamma[h] * rel), 0.0)
        o = o + jnp.where(hl == h, _dot(inner.astype(BF16), v), 0.0)

    pos = lax.broadcasted_iota(jnp.int32, (chunk, 1), 0).astype(F32)
    st = state_ref[...]
    q_dec = (q.astype(F32) * jnp.exp(lg * (pos + 1.0))).astype(BF16)
    o = o + _dot(q_dec, st.astype(BF16))
    k_dec = (k.astype(F32) * jnp.exp(lg * (chunk - 1.0 - pos))).astype(BF16)
    r2 = lax.broadcasted_iota(jnp.int32, (gw, gw), 0) >> 6
    c2 = lax.broadcasted_iota(jnp.int32, (gw, gw), 1) >> 6
    state_ref[...] = jnp.where(r2 == c2, st * jnp.exp(lg * float(chunk)) + _dot_tn(k_dec, v), 0.0)

    bd = bd_ref[...]
    oc = o - _group_mean(o, bd)
    y = oc * lax.rsqrt(_group_mean(oc * oc, bd) + EPS) * gain_ref[...]
    g = g_ref[...].astype(F32)
    o_ref[...] = (y * (g / (1.0 + jnp.exp(-g)))).astype(BF16)


def _retention(qa, ka, va, ga, gain, bd, batch, seq, chunk):
    t, gw = qa.shape
    n = seq // chunk
    blk = pl.BlockSpec((chunk, gw), lambda b, c: (b * n + c, 0))
    return pl.pallas_call(
        functools.partial(_retention_body, chunk=chunk),
        grid=(batch, n),
        in_specs=[blk, blk, blk, blk, _full(gain.shape), _full(bd.shape)],
        out_specs=blk,
        out_shape=jax.ShapeDtypeStruct((t, gw), BF16),
        scratch_shapes=[pltpu.VMEM((gw, gw), F32)],
        compiler_params=_params(("arbitrary", "arbitrary")),
        name="retention",
    )(qa, ka, va, ga, gain, bd)


def _head_of_row(width):
    r = lax.broadcasted_iota(jnp.int32, (width, 1), 0)
    if width == GROUP_WIDTH:
        return r >> 6
    return jnp.where(r < GROUP_WIDTH, r >> 6, (r - GROUP_WIDTH) >> 5)


def _mask_queries(qt_ref, qm_ref):
    qt = qt_ref[...]
    hr = _head_of_row(qt.shape[0])
    for h in range(N_HEADS):
        qm_ref[h] = jnp.where(hr == h, qt, jnp.zeros_like(qt))


def _finish_heads(acc_ref, scale_of_head, bd, gain):
    ot = jnp.concatenate([acc_ref[h] * scale_of_head(h) for h in range(N_HEADS)], axis=0)
    o = ot.T
    return o * lax.rsqrt(_group_mean(o * o, bd) + EPS) * gain


def _softmax_attn_body(*refs, tq, has_bias):
    if has_bias:
        qt_ref, k_ref, vt_ref, bias_ref, gain_ref, bd_ref, o_ref, qm_ref, m_ref, l_ref, acc_ref = refs
    else:
        qt_ref, k_ref, vt_ref, gain_ref, bd_ref, o_ref, qm_ref, m_ref, l_ref, acc_ref = refs
        bias_ref = None
    qi = pl.program_id(1)
    _mask_queries(qt_ref, qm_ref)
    m_ref[...] = jnp.full_like(m_ref, NEG_INF)
    l_ref[...] = jnp.zeros_like(l_ref)
    acc_ref[...] = jnp.zeros_like(acc_ref)
    key = lax.broadcasted_iota(jnp.int32, (tq, tq), 0)
    qry = lax.broadcasted_iota(jnp.int32, (tq, tq), 1)

    def block(j, diagonal):
        off = pl.multiple_of(j * tq, tq)
        kb = k_ref[pl.ds(off, tq), :]
        scores = [_dot(kb, qm_ref[h]) for h in range(N_HEADS)]
        for h in range(N_HEADS):
            s = scores[h]
            if has_bias:
                kbias = bias_ref[pl.ds(off, tq), h * LANES:(h + 1) * LANES]
                s = s - jnp.concatenate([kbias] * (tq // LANES), axis=1)
            if diagonal:
                s = jnp.where(key <= qry, s, NEG_INF)
            m_prev = m_ref[h]
            m_new = jnp.maximum(m_prev, jnp.max(s, axis=0, keepdims=True))
            alpha = jnp.exp(m_prev - m_new)
            p = jnp.exp(s - m_new)
            l_ref[h] = alpha * l_ref[h] + jnp.sum(p, axis=0, keepdims=True)
            vt = vt_ref[h * HEAD_DIM:(h + 1) * HEAD_DIM, pl.ds(off, tq)]
            acc_ref[h] = alpha * acc_ref[h] + _dot(vt, p.astype(BF16))
            m_ref[h] = m_new

    def body(j, carry):
        block(j, False)
        return carry

    lax.fori_loop(0, qi, body, 0)
    block(qi, True)
    y = _finish_heads(acc_ref, lambda h: 1.0 / l_ref[h], bd_ref[...], gain_ref[...])
    o_ref[...] = y.astype(BF16)


def _softmax_attn(qt, k, vt, bias, gain, bd, batch, seq, tq):
    wq, t = qt.shape
    gw = GROUP_WIDTH
    nq = seq // tq
    has_bias = bias is not None
    in_specs = [pl.BlockSpec((wq, tq), lambda b, i: (0, b * nq + i)),
                pl.BlockSpec((seq, wq), lambda b, i: (b, 0)),
                pl.BlockSpec((gw, seq), lambda b, i: (0, b))]
    args = [qt, k, vt]
    if has_bias:
        in_specs.append(pl.BlockSpec((seq, bias.shape[1]), lambda b, i: (b, 0)))
        args.append(bias)
    in_specs += [_full(gain.shape), _full(bd.shape)]
    args += [gain, bd]
    return pl.pallas_call(
        functools.partial(_softmax_attn_body, tq=tq, has_bias=has_bias),
        grid=(batch, nq),
        in_specs=in_specs,
        out_specs=pl.BlockSpec((tq, gw), lambda b, i: (b * nq + i, 0)),
        out_shape=jax.ShapeDtypeStruct((t, gw), BF16),
        scratch_shapes=[pltpu.VMEM((N_HEADS, wq, tq), BF16),
                        pltpu.VMEM((N_HEADS, 1, tq), F32),
                        pltpu.VMEM((N_HEADS, 1, tq), F32),
                        pltpu.VMEM((N_HEADS, HEAD_DIM, tq), F32)],
        compiler_params=_params(("arbitrary", "arbitrary")),
        name="fox_attn" if has_bias else "mla_attn",
    )(*args)


def _stickbreak_body(qt_ref, k_ref, vt_ref, gain_ref, bd_ref, tri_ref, o_ref,
                     qm_ref, carry_ref, acc_ref, *, tq):
    qi = pl.program_id(1)
    _mask_queries(qt_ref, qm_ref)
    carry_ref[...] = jnp.zeros_like(carry_ref)
    acc_ref[...] = jnp.zeros_like(acc_ref)
    key = lax.broadcasted_iota(jnp.int32, (tq, tq), 0)
    qry = lax.broadcasted_iota(jnp.int32, (tq, tq), 1)

    def block(j, diagonal):
        off = pl.multiple_of(j * tq, tq)
        kb = k_ref[pl.ds(off, tq), :]
        tri = tri_ref[...]
        zs = [_dot(kb, qm_ref[h]) for h in range(N_HEADS)]
        log_betas, afters = [], []
        for h in range(N_HEADS):
            log_beta = _log_sigmoid(zs[h])
            log_rest = log_beta - zs[h]
            if diagonal:
                log_rest = jnp.where(key < qry, log_rest, 0.0)
            hi = log_rest.astype(BF16)
            lo = (log_rest - hi.astype(F32)).astype(BF16)
            afters.append(_dot(tri, hi) + _dot(tri, lo) + carry_ref[h])
            log_betas.append(log_beta)
            carry_ref[h] = carry_ref[h] + jnp.sum(log_rest, axis=0, keepdims=True)
        for h in range(N_HEADS):
            w = jnp.exp(log_betas[h] + afters[h])
            if diagonal:
                w = jnp.where(key < qry, w, 0.0)
            vt = vt_ref[h * HEAD_DIM:(h + 1) * HEAD_DIM, pl.ds(off, tq)]
            acc_ref[h] = acc_ref[h] + _dot(vt, w.astype(BF16))

    block(qi, True)

    def body(it, carry):
        block(qi - 1 - it, False)
        return carry

    lax.fori_loop(0, qi, body, 0)
    y = _finish_heads(acc_ref, lambda h: 1.0, bd_ref[...], gain_ref[...])
    o_ref[...] = y.astype(BF16)


def _stickbreak(qt, k, vt, gain, bd, batch, seq, tq):
    gw, t = qt.shape
    nq = seq // tq
    tri = (jnp.arange(tq)[None, :] > jnp.arange(tq)[:, None]).astype(BF16)
    return pl.pallas_call(
        functools.partial(_stickbreak_body, tq=tq),
        grid=(batch, nq),
        in_specs=[pl.BlockSpec((gw, tq), lambda b, i: (0, b * nq + i)),
                  pl.BlockSpec((seq, gw), lambda b, i: (b, 0)),
                  pl.BlockSpec((gw, seq), lambda b, i: (0, b)),
                  _full(gain.shape), _full(bd.shape), _full(tri.shape)],
        out_specs=pl.BlockSpec((tq, gw), lambda b, i: (b * nq + i, 0)),
        out_shape=jax.ShapeDtypeStruct((t, gw), BF16),
        scratch_shapes=[pltpu.VMEM((N_HEADS, gw, tq), BF16),
                        pltpu.VMEM((N_HEADS, 1, tq), F32),
                        pltpu.VMEM((N_HEADS, HEAD_DIM, tq), F32)],
        compiler_params=_params(("arbitrary", "arbitrary")),
        name="stickbreak_attn",
    )(qt, k, vt, gain, bd, tri)


def _outproj_body(x_ref, ya_ref, yb_ref, yc_ref, yd_ref, wo_ref, gf_ref, wq_ref, sk_ref,
                  x1_ref, h2_ref, sc_ref):
    gw = GROUP_WIDTH
    acc = x_ref[...]
    for n, y_ref in enumerate((ya_ref, yb_ref, yc_ref, yd_ref)):
        acc = acc + _dot(y_ref[...], wo_ref[n * gw:(n + 1) * gw, :])
    x1_ref[...] = acc
    h2 = _rms(acc, gf_ref[...]).astype(BF16)
    h2_ref[...] = h2
    q = _dot(h2, wq_ref[...])
    for hp in range(sc_ref.shape[0]):
        sc_ref[hp] = _dot_nt(sk_ref[hp], q[:, hp * LANES:(hp + 1) * LANES].astype(BF16))


def _outproj(x2, ys, wo, gf, wq, sk, tm):
    t, d = x2.shape
    gw = GROUP_WIDTH
    nk = sk.shape[0]
    tok = lambda w: pl.BlockSpec((tm, w), lambda i: (i, 0))
    return pl.pallas_call(
        _outproj_body,
        grid=(t // tm,),
        in_specs=[tok(d)] + [tok(gw)] * 4 + [_full(wo.shape), _full(gf.shape), _full(wq.shape), _full(sk.shape)],
        out_specs=[tok(d), tok(d), pl.BlockSpec((nk, PEER_KEYS, tm), lambda i: (0, 0, i))],
        out_shape=[jax.ShapeDtypeStruct((t, d), F32), jax.ShapeDtypeStruct((t, d), BF16),
                   jax.ShapeDtypeStruct((nk, PEER_KEYS, t), F32)],
        compiler_params=_params(("arbitrary",)),
        name="outproj_peerq",
    )(x2, *ys, wo, gf, wq, sk)


def _top_k_rows(s, k, big=None):
    rows = lax.broadcasted_iota(jnp.int32, s.shape, 0).astype(F32)
    rank = None if big is None else jnp.full(s.shape, big, F32)
    vals, idxs = [], []
    for r in range(k):
        m = jnp.max(s, axis=0, keepdims=True)
        idx = jnp.min(jnp.where(s == m, rows, float(s.shape[0])), axis=0, keepdims=True)
        hit = rows == idx
        s = jnp.where(hit, NEG_INF, s)
        if rank is not None:
            rank = jnp.where(hit, float(r), rank)
        vals.append(m)
        idxs.append(idx)
    return vals, idxs, rank


def _route_body(sc_ref, rank2_ref, cnt_ref, e1_ref, e2_ref, cand_ref):
    k = PEER_TOPK
    not_selected = float(2 * k)

    def head(h, carry):
        s1 = sc_ref[2 * h]
        s2 = sc_ref[2 * h + 1]
        v1, idx1, _ = _top_k_rows(s1, k)
        v2, _, rank2 = _top_k_rows(s2, k, not_selected)
        cand_ref[...] = jnp.full(cand_ref.shape, NEG_INF, F32)
        for p, (a, b) in enumerate(PEER_PAIRS):
            cand_ref[p:p + 1, :] = v1[a] + v2[b]
        cand = cand_ref[...]
        _, _, crank = _top_k_rows(cand, k, not_selected)
        sel = crank < float(k)
        top = v1[0] + v2[0]
        z = jnp.sum(jnp.where(sel, jnp.exp(cand - top), 0.0), axis=0, keepdims=True)
        prow = lax.broadcasted_iota(jnp.int32, cand.shape, 0)
        keys = lax.broadcasted_iota(jnp.int32, s1.shape, 0).astype(F32)
        cnt = jnp.zeros(s1.shape, F32)
        row0 = 0
        for a in range(k):
            na = k // (a + 1)
            in_a = (prow >= row0) & (prow < row0 + na) & sel
            n_a = jnp.sum(jnp.where(in_a, 1.0, 0.0), axis=0, keepdims=True)
            cnt = cnt + jnp.where(keys == idx1[a], n_a, 0.0)
            row0 += na
        rank2_ref[h] = rank2.astype(BF16)
        cnt_ref[h] = cnt
        e1_ref[h] = jnp.exp(s1 - v1[0]) * (0.5 / z)
        e2_ref[h] = jnp.exp(s2 - v2[0]).astype(BF16)
        return carry

    lax.fori_loop(0, PEER_HEADS, head, 0)


def _route(scores_t, tr):
    nk, nkeys, t = scores_t.shape
    out = lambda dt: jax.ShapeDtypeStruct((PEER_HEADS, nkeys, t), dt)
    spec = pl.BlockSpec((PEER_HEADS, nkeys, tr), lambda i: (0, 0, i))
    return pl.pallas_call(
        _route_body,
        grid=(t // tr,),
        in_specs=[pl.BlockSpec((nk, nkeys, tr), lambda i: (0, 0, i))],
        out_specs=[spec] * 4,
        out_shape=[out(BF16), out(F32), out(F32), out(BF16)],
        scratch_shapes=[pltpu.VMEM((PEER_PAIR_ROWS, tr), F32)],
        compiler_params=_params(("arbitrary",)),
        name="peer_route",
    )(scores_t)


def _peer_body(h_ref, u_ref, vt_ref, rank2_ref, cnt_ref, e1_ref, e2_ref, x1_ref, gfin_ref, o_ref,
               acc_ref, s0_ref, s1_ref, w_ref, *, groups, final_norm):
    n = pl.program_id(1)
    n_slabs = pl.num_programs(1) - 1
    nk = PEER_KEYS
    tm = h_ref.shape[0]
    s_ref = (s0_ref, s1_ref)

    @pl.when(n == 0)
    def _():
        acc_ref[...] = jnp.zeros_like(acc_ref)
        s1_ref[...] = jnp.zeros_like(s1_ref)

    def step(p):
        q = 1 - p
        s_ref[p][...] = _dot_nt(u_ref[...], h_ref[...])
        prev = jnp.clip(n - 1, 0, n_slabs - 1)
        for lanes in (slice(0, tm // 2), slice(tm // 2, tm)):
            for g in range(groups):
                i = prev * groups + g
                s = s_ref[q][g * nk:(g + 1) * nk, lanes]
                act = s * (1.0 + lax.erf(s * (2.0 ** -0.5)))
                gate = None
                for h in range(PEER_HEADS):
                    chosen = rank2_ref[h, :, lanes] < cnt_ref[h, pl.ds(i, 1), lanes].astype(BF16)
                    term = (jnp.where(chosen, e2_ref[h, :, lanes], jnp.zeros((), BF16))
                            * e1_ref[h, pl.ds(i, 1), lanes].astype(BF16))
                    gate = term if gate is None else gate + term
                w_ref[g * nk:(g + 1) * nk, lanes] = gate * act.astype(BF16)
            acc_ref[:, lanes] += _dot(vt_ref[...], w_ref[:, lanes])

    for p in range(2):
        pl.when(n % 2 == p)(functools.partial(step, p))

    @pl.when(n == pl.num_programs(1) - 1)
    def _():
        y = x1_ref[...] + acc_ref[...].T
        if final_norm:
            y = _rms(y, gfin_ref[...])
        o_ref[...] = y


def _peer(h2, u, v, rank2, cnt, e1, e2, x1, gfin, final_norm, tm, groups):
    t, d = h2.shape
    ne = u.shape[0]
    slab = groups * PEER_KEYS
    n_slabs = ne // slab
    vt = v.reshape(n_slabs, slab, d).transpose(0, 2, 1)
    tok = pl.BlockSpec((tm, d), lambda i, s: (i, 0))
    route = pl.BlockSpec((PEER_HEADS, PEER_KEYS, tm), lambda i, s: (0, 0, i))
    return pl.pallas_call(
        functools.partial(_peer_body, groups=groups, final_norm=final_norm),
        grid=(t // tm, n_slabs + 1),
        in_specs=[tok, pl.BlockSpec((slab, d), lambda i, s: (jnp.minimum(s, n_slabs - 1), 0)),
                  pl.BlockSpec((None, d, slab), lambda i, s: (jnp.maximum(s - 1, 0), 0, 0)),
                  route, route, route, route, tok, _full(gfin.shape)],
        out_specs=tok,
        out_shape=jax.ShapeDtypeStruct((t, d), F32),
        scratch_shapes=[pltpu.VMEM((d, tm), F32), pltpu.VMEM((slab, tm), F32), pltpu.VMEM((slab, tm), F32),
                        pltpu.VMEM((slab, tm), BF16)],
        compiler_params=_params(("arbitrary", "arbitrary")),
        name="peer_experts",
    )(h2, u, vt, rank2, cnt, e1, e2, x1, gfin)


def _rope_tables(seq, dim, reps):
    pos = jnp.arange(seq, dtype=F32)
    inv = ROPE_BASE ** (-jnp.arange(0, dim, 2, dtype=F32) / dim)
    ang = pos[:, None] * inv[None, :]
    cos, sin = jnp.cos(ang), jnp.sin(ang)
    return (jnp.tile(jnp.concatenate([cos, cos], axis=1), (1, reps)),
            jnp.tile(jnp.concatenate([-sin, sin], axis=1), (1, reps)), cos.T, sin.T)


def _tile_sizes(seq, tokens):
    pick = lambda n, pref: pref if n % pref == 0 else n
    return dict(tm=pick(seq, 512), chunk=pick(seq, 256), tq=pick(seq, 256),
                tr=pick(tokens, 256), tp=pick(tokens, 512), groups=4)


def kernel(x, norm_mix, w_in, b_forget, norm_cq, norm_ckv, w_uq, w_ukv, head_gain, w_out, norm_ffn, w_query, sub_keys, expert_u, expert_v, norm_final):
    batch, seq, d = x.shape
    depth = w_in.shape[0]
    t = batch * seq
    gw = GROUP_WIDTH
    ts = _tile_sizes(seq, t)
    c64, s64, _, _ = _rope_tables(seq, HEAD_DIM, N_HEADS)
    c32, s32, c32t, s32t = _rope_tables(seq, MLA_ROPE, N_HEADS)
    blk = jnp.arange(gw) // HEAD_DIM
    bd = jnp.where(blk[:, None] == blk[None, :], 1.0 / HEAD_DIM, 0.0).astype(BF16)
    row = lambda v: v.reshape(1, -1).astype(F32)
    pad_lanes = lambda a: jnp.pad(a, ((0, 0), (0, LANES - a.shape[1])))

    x2 = x.reshape(t, d)
    for l in range(depth):
        w = w_in[l]
        cols = lambda k: w[:, k * gw:(k + 1) * gw]
        o_ff = 7 * gw
        o_s = o_ff + N_HEADS
        scols = lambda k: w[:, o_s + k * gw:o_s + (k + 1) * gw]
        o_cq = o_s + 3 * gw
        o_ckv = o_cq + w_uq.shape[1]
        o_kr = o_ckv + w_ukv.shape[1]
        w_main = jnp.concatenate(
            [w[:, :4 * gw], cols(5), scols(1), w[:, o_cq:o_kr], jnp.tile(w[:, o_kr:o_kr + MLA_ROPE], (1, N_HEADS)),
             pad_lanes(w[:, o_ff:o_s])], axis=1).astype(BF16)
        w_t = jnp.concatenate([cols(4), cols(6), scols(0), scols(2)], axis=1).T.astype(BF16)
        bf = pad_lanes(b_forget[l].reshape(1, -1).astype(F32))
        uq = w_uq[l].reshape(-1, N_HEADS, MLA_NOPE + MLA_ROPE)
        wuq_t = jnp.concatenate([uq[:, :, :MLA_NOPE].reshape(-1, gw),
                                 uq[:, :, MLA_NOPE:].reshape(-1, N_HEADS * MLA_ROPE)], axis=1).T.astype(BF16)
        ukv = w_ukv[l].reshape(-1, N_HEADS, MLA_NOPE + HEAD_DIM)
        wukv_n = ukv[:, :, :MLA_NOPE].reshape(-1, gw).astype(BF16)
        wukv_vt = ukv[:, :, MLA_NOPE:].reshape(-1, gw).T.astype(BF16)
        (qa, ka, va, ga, fk, sk, kd, fqt, fvt, sqt, svt, qdt, vdt, fbias) = _inproj(
            x2, row(norm_mix[l]), w_main, w_t, bf, row(norm_cq[l]), row(norm_ckv[l]), wuq_t, wukv_n, wukv_vt,
            c64, s64, c32, s32, c32t, s32t, seq, ts["tm"])
        gain = head_gain[l].astype(F32)
        ya = _retention(qa, ka, va, ga, row(gain[0:gw]), bd, batch, seq, ts["chunk"])
        yb = _softmax_attn(fqt, fk, fvt, fbias, row(gain[gw:2 * gw]), bd, batch, seq, ts["tq"])
        yc = _stickbreak(sqt, sk, svt, row(gain[2 * gw:3 * gw]), bd, batch, seq, ts["tq"])
        yd = _softmax_attn(qdt, kd, vdt, None, row(gain[3 * gw:4 * gw]), bd, batch, seq, ts["tq"])
        keys = sub_keys[l].reshape(2 * PEER_HEADS, PEER_KEYS, -1).astype(BF16)
        x1, h2, scores_t = _outproj(x2, (ya, yb, yc, yd), w_out[l].astype(BF16), row(norm_ffn[l]),
                                    w_query[l].astype(BF16), keys, ts["tm"])
        rank2, cnt, e1, e2 = _route(scores_t, ts["tr"])
        x2 = _peer(h2, expert_u[l].astype(BF16), expert_v[l].astype(BF16), rank2, cnt, e1, e2, x1,
                   row(norm_final), l == depth - 1, ts["tp"], ts["groups"])
    return x2.reshape(batch, seq, d)
```

```python
import functools
import math

import numpy as np
import jax
import jax.numpy as jnp
from jax import lax
from jax.experimental import pallas as pl
from jax.experimental.pallas import tpu as pltpu

F32 = jnp.float32
BF16 = jnp.bfloat16

N_HEADS = 4
HEAD_DIM = 64
GROUP_WIDTH = N_HEADS * HEAD_DIM
MLA_NOPE = 64
MLA_ROPE = 32
ROPE_BASE = 10000.0
PEER_HEADS = 8
PEER_KEYS = 128
PEER_TOPK = 16
EPS = 1e-6
NEG_INF = float("-inf")

LANES = 128
VMEM_LIMIT = 56 * 1024 * 1024

PEER_PAIRS = tuple((a, b) for a in range(PEER_TOPK) for b in range(PEER_TOPK // (a + 1)))
PEER_PAIR_ROWS = -(-len(PEER_PAIRS) // 8) * 8

_NT = (((1,), (1,)), ((), ()))
_TN = (((0,), (0,)), ((), ()))


def _dot(a, b):
    return jnp.dot(a, b, preferred_element_type=F32)


def _dot_nt(a, b):
    return lax.dot_general(a, b, _NT, preferred_element_type=F32)


def _dot_tn(a, b):
    return lax.dot_general(a, b, _TN, preferred_element_type=F32)


def _split_dot(x, m, terms):
    acc = None
    r = x
    for t in range(terms):
        p = r.astype(BF16)
        d = _dot(p, m)
        acc = d if acc is None else acc + d
        if t + 1 < terms:
            r = r - p.astype(F32)
    return acc


def _log_sigmoid(x):
    return jnp.minimum(x, 0.0) - jnp.log(1.0 + jnp.exp(-jnp.abs(x)))


def _rms(x, g):
    return x * lax.rsqrt(jnp.mean(x * x, axis=-1, keepdims=True) + EPS) * g


def _rope(x, c, s_signed, half):
    outs = []
    for o in range(0, x.shape[1], LANES):
        xs = x[:, o:o + LANES]
        lane = lax.broadcasted_iota(jnp.int32, xs.shape, 1)
        first = (lane & (2 * half - 1)) < half
        rot = jnp.where(first, pltpu.roll(xs, LANES - half, 1), pltpu.roll(xs, half, 1))
        outs.append(xs * c[:, o:o + LANES] + rot * s_signed[:, o:o + LANES])
    return outs[0] if len(outs) == 1 else jnp.concatenate(outs, axis=1)


def _head_of_lane(width):
    lane = lax.broadcasted_iota(jnp.int32, (1, width), 1)
    if width == GROUP_WIDTH:
        return lane >> 6
    return jnp.where(lane < GROUP_WIDTH, lane >> 6, (lane - GROUP_WIDTH) >> 5)


def _params(semantics, flags=None):
    return pltpu.CompilerParams(dimension_semantics=semantics, vmem_limit_bytes=VMEM_LIMIT, flags=flags)


def _full(shape):
    nd = len(shape)
    return pl.BlockSpec(shape, lambda *_: (0,) * nd)


def _inproj_body(x_ref, g_ref, w_ref, wt_ref, bf_ref, ncq_ref, nckv_ref, wuqt_ref, wukv_ref, wukvt_ref,
                 c64_ref, s64_ref, c32_ref, s32_ref, c32t_ref, s32t_ref, tril_ref, rep_ref,
                 qa_ref, ka_ref, va_ref, ga_ref, fk_ref, sk_ref, kd_ref,
                 fqt_ref, fvt_ref, sqt_ref, svt_ref, qdt_ref, vdt_ref, fb_ref, carry_ref, *, tiles_per_seq):
    i = pl.program_id(0)
    hb = _rms(x_ref[...], g_ref[...]).astype(BF16)
    gw = GROUP_WIDTH

    def proj(k, n=gw):
        return _dot(hb, w_ref[:, k * gw:k * gw + n])

    c64, s64 = c64_ref[...], s64_ref[...]
    c32, s32 = c32_ref[...], s32_ref[...]
    scale = HEAD_DIM ** -0.5
    qa_ref[...] = _rope(proj(0), c64, s64, HEAD_DIM // 2).astype(BF16)
    ka_ref[...] = (_rope(proj(1), c64, s64, HEAD_DIM // 2) * scale).astype(BF16)
    va_ref[...] = proj(2).astype(BF16)
    ga_ref[...] = proj(3).astype(BF16)
    fk_ref[...] = proj(4).astype(BF16)
    sk_ref[...] = proj(5).astype(BF16)

    tt = _dot_nt(wt_ref[...], hb)
    fqt_ref[...] = (tt[0:gw] * scale).astype(BF16)
    fvt_ref[...] = tt[gw:2 * gw].astype(BF16)
    sqt_ref[...] = (tt[2 * gw:3 * gw] * scale).astype(BF16)
    svt_ref[...] = tt[3 * gw:4 * gw].astype(BF16)

    mla_scale = (MLA_NOPE + MLA_ROPE) ** -0.5
    cq = _rms(proj(6), ncq_ref[...]).astype(BF16)
    qt = _dot_nt(wuqt_ref[...], cq) * mla_scale
    qdt_ref[0:gw, :] = qt[0:gw].astype(BF16)
    half = MLA_ROPE // 2
    ct, st = c32t_ref[...], s32t_ref[...]
    for h in range(N_HEADS):
        r0 = gw + h * MLA_ROPE
        t1, t2 = qt[r0:r0 + half], qt[r0 + half:r0 + 2 * half]
        qdt_ref[r0:r0 + half, :] = (t1 * ct - t2 * st).astype(BF16)
        qdt_ref[r0 + half:r0 + 2 * half, :] = (t1 * st + t2 * ct).astype(BF16)
    ckv = _rms(proj(7, LANES), nckv_ref[...]).astype(BF16)
    kd_ref[:, 0:gw] = _dot(ckv, wukv_ref[...]).astype(BF16)
    vdt_ref[...] = _dot_nt(wukvt_ref[...], ckv).astype(BF16)
    kr = _dot(hb, w_ref[:, 7 * gw + LANES:7 * gw + 2 * LANES])
    kd_ref[:, gw:gw + LANES] = _rope(kr, c32, s32, MLA_ROPE // 2).astype(BF16)

    lf = _log_sigmoid(proj(8, LANES) + bf_ref[...])
    tril = tril_ref[...]
    cum = None
    r = lf
    for term in range(3):
        part = r.astype(BF16)
        d = _dot(tril, part)
        cum = d if cum is None else cum + d
        if term < 2:
            r = r - part.astype(F32)

    @pl.when(i % tiles_per_seq == 0)
    def _():
        carry_ref[...] = jnp.zeros_like(carry_ref)

    fc = cum + carry_ref[0:1, :]
    tm = fc.shape[0]
    carry_ref[...] = jnp.broadcast_to(fc[tm - 1:tm, :], carry_ref.shape)
    fb_ref[...] = _split_dot(fc, rep_ref[...], 3)


def _inproj(x2, g, w_main, w_t, bf, ncq, nckv, wuq_t, wukv_n, wukv_vt, c64, s64, c32, s32, c32t, s32t, seq, tm):
    t, d = x2.shape
    gw = GROUP_WIDTH
    tiles_per_seq = seq // tm
    tril = (jnp.arange(tm)[:, None] >= jnp.arange(tm)[None, :]).astype(BF16)
    lane_head = jnp.arange(N_HEADS * LANES) // LANES
    rep = (jnp.arange(LANES)[:, None] == lane_head[None, :]).astype(BF16)
    tok = lambda w: pl.BlockSpec((tm, w), lambda i: (i, 0))
    chan = lambda c: pl.BlockSpec((c, tm), lambda i: (0, i))
    pos = lambda w: pl.BlockSpec((tm, w), lambda i: (i % tiles_per_seq, 0))
    pos_t = pl.BlockSpec((MLA_ROPE // 2, tm), lambda i: (0, i % tiles_per_seq))
    tok_widths = [gw] * 6 + [gw + LANES]
    chan_rows = [gw] * 4 + [gw + LANES, gw]
    out_shape = ([jax.ShapeDtypeStruct((t, w), BF16) for w in tok_widths]
                 + [jax.ShapeDtypeStruct((c, t), BF16) for c in chan_rows]
                 + [jax.ShapeDtypeStruct((t, N_HEADS * LANES), F32)])
    out_specs = [tok(w) for w in tok_widths] + [chan(c) for c in chan_rows] + [tok(N_HEADS * LANES)]
    return pl.pallas_call(
        functools.partial(_inproj_body, tiles_per_seq=tiles_per_seq),
        grid=(t // tm,),
        in_specs=[tok(d), _full(g.shape), _full(w_main.shape), _full(w_t.shape), _full(bf.shape),
                  _full(ncq.shape), _full(nckv.shape), _full(wuq_t.shape),
                  _full(wukv_n.shape), _full(wukv_vt.shape),
                  pos(gw), pos(gw), pos(LANES), pos(LANES), pos_t, pos_t, _full(tril.shape), _full(rep.shape)],
        out_specs=out_specs,
        out_shape=out_shape,
        scratch_shapes=[pltpu.VMEM((8, LANES), F32)],
        compiler_params=_params(("arbitrary",)),
        name="inproj",
    )(x2, g, w_main, w_t, bf, ncq, nckv, wuq_t, wukv_n, wukv_vt, c64, s64, c32, s32, c32t, s32t, tril, rep)


def _group_mean(x, bd):
    return _split_dot(x, bd, 2)


def _retention_body(q_ref, k_ref, v_ref, g_ref, gain_ref, bd_ref, o_ref, state_ref, *, chunk):
    ci = pl.program_id(1)

    @pl.when(ci == 0)
    def _():
        state_ref[...] = jnp.zeros_like(state_ref)

    gw = GROUP_WIDTH
    log_gamma = [math.log1p(-(2.0 ** (-5.0 - h))) for h in range(N_HEADS)]
    hl = _head_of_lane(gw)
    lg = jnp.zeros((1, gw), F32)
    for h in range(N_HEADS):
        lg = jnp.where(hl == h, log_gamma[h], lg)

    q, k, v = q_ref[...], k_ref[...], v_ref[...]
    ri = lax.broadcasted_iota(jnp.int32, (chunk, chunk), 0)
    cj = lax.broadcasted_iota(jnp.int32, (chunk, chunk), 1)
    rel = jnp.maximum(ri - cj, 0).astype(F32)
    causal = ri >= cj
    o = jnp.zeros((chunk, gw), F32)
    for h in range(N_HEADS):
        qm = jnp.where(hl == h, q, jnp.zeros_like(q))
        inner = _dot_nt(qm, k) * jnp.where(causal, jnp.exp(log_gamma[h] * rel), 0.0)
        o = o + jnp.where(hl == h, _dot(inner.astype(BF16), v), 0.0)

    pos = lax.broadcasted_iota(jnp.int32, (chunk, 1), 0).astype(F32)
    st = state_ref[...]
    q_dec = (q.astype(F32) * jnp.exp(lg * (pos + 1.0))).astype(BF16)
    o = o + _dot(q_dec, st.astype(BF16))
    k_dec = (k.astype(F32) * jnp.exp(lg * (chunk - 1.0 - pos))).astype(BF16)
    r2 = lax.broadcasted_iota(jnp.int32, (gw, gw), 0) >> 6
    c2 = lax.broadcasted_iota(jnp.int32, (gw, gw), 1) >> 6
    state_ref[...] = jnp.where(r2 == c2, st * jnp.exp(lg * float(chunk)) + _dot_tn(k_dec, v), 0.0)

    bd = bd_ref[...]
    oc = o - _group_mean(o, bd)
    y = oc * lax.rsqrt(_group_mean(oc * oc, bd) + EPS) * gain_ref[...]
    g = g_ref[...].astype(F32)
    o_ref[...] = (y * (g / (1.0 + jnp.exp(-g)))).astype(BF16)


def _retention(qa, ka, va, ga, gain, bd, batch, seq, chunk):
    t, gw = qa.shape
    n = seq // chunk
    blk = pl.BlockSpec((chunk, gw), lambda b, c: (b * n + c, 0))
    return pl.pallas_call(
        functools.partial(_retention_body, chunk=chunk),
        grid=(batch, n),
        in_specs=[blk, blk, blk, blk, _full(gain.shape), _full(bd.shape)],
        out_specs=blk,
        out_shape=jax.ShapeDtypeStruct((t, gw), BF16),
        scratch_shapes=[pltpu.VMEM((gw, gw), F32)],
        compiler_params=_params(("arbitrary", "arbitrary")),
        name="retention",
    )(qa, ka, va, ga, gain, bd)


def _head_of_row(width):
    r = lax.broadcasted_iota(jnp.int32, (width, 1), 0)
    if width == GROUP_WIDTH:
        return r >> 6
    return jnp.where(r < GROUP_WIDTH, r >> 6, (r - GROUP_WIDTH) >> 5)


def _mask_queries(qt_ref, qm_ref):
    qt = qt_ref[...]
    hr = _head_of_row(qt.shape[0])
    for h in range(N_HEADS):
        qm_ref[h] = jnp.where(hr == h, qt, jnp.zeros_like(qt))


def _finish_heads(acc_ref, scale_of_head, bd, gain):
    ot = jnp.concatenate([acc_ref[h] * scale_of_head(h) for h in range(N_HEADS)], axis=0)
    o = ot.T
    return o * lax.rsqrt(_group_mean(o * o, bd) + EPS) * gain


def _softmax_attn_body(*refs, tq, has_bias):
    if has_bias:
        qt_ref, k_ref, vt_ref, bias_ref, gain_ref, bd_ref, o_ref, qm_ref, m_ref, l_ref, acc_ref = refs
    else:
        qt_ref, k_ref, vt_ref, gain_ref, bd_ref, o_ref, qm_ref, m_ref, l_ref, acc_ref = refs
        bias_ref = None
    qi = pl.program_id(1)
    _mask_queries(qt_ref, qm_ref)
    m_ref[...] = jnp.full_like(m_ref, NEG_INF)
    l_ref[...] = jnp.zeros_like(l_ref)
    acc_ref[...] = jnp.zeros_like(acc_ref)
    key = lax.broadcasted_iota(jnp.int32, (tq, tq), 0)
    qry = lax.broadcasted_iota(jnp.int32, (tq, tq), 1)

    def blocks(js, last_is_diagonal):
        offs = [pl.multiple_of(j * tq, tq) for j in js]
        scores = [[_dot(k_ref[pl.ds(off, tq), :], qm_ref[h]) for h in range(N_HEADS)] for off in offs]
        for n, off in enumerate(offs):
            diagonal = last_is_diagonal and n == len(offs) - 1
            for h in range(N_HEADS):
                s = scores[n][h]
                if has_bias:
                    kbias = bias_ref[pl.ds(off, tq), h * LANES:(h + 1) * LANES]
                    s = s - jnp.concatenate([kbias] * (tq // LANES), axis=1)
                if diagonal:
                    s = jnp.where(key <= qry, s, NEG_INF)
                m_prev = m_ref[h]
                m_new = jnp.maximum(m_prev, jnp.max(s, axis=0, keepdims=True))
                alpha = jnp.exp(m_prev - m_new)
                p = jnp.exp(s - m_new)
                l_ref[h] = alpha * l_ref[h] + jnp.sum(p, axis=0, keepdims=True)
                vt = vt_ref[h * HEAD_DIM:(h + 1) * HEAD_DIM, pl.ds(off, tq)]
                acc_ref[h] = alpha * acc_ref[h] + _dot(vt, p.astype(BF16))
                m_ref[h] = m_new

    def body(j, carry):
        blocks([2 * j, 2 * j + 1], False)
        return carry

    lax.fori_loop(0, qi // 2, body, 0)

    @pl.when(qi % 2 == 1)
    def _():
        blocks([qi - 1, qi], True)

    @pl.when(qi % 2 == 0)
    def _():
        blocks([qi], True)

    y = _finish_heads(acc_ref, lambda h: 1.0 / l_ref[h], bd_ref[...], gain_ref[...])
    o_ref[...] = y.astype(BF16)


def _softmax_attn(qt, k, vt, bias, gain, bd, batch, seq, tq):
    wq, t = qt.shape
    gw = GROUP_WIDTH
    nq = seq // tq
    has_bias = bias is not None
    in_specs = [pl.BlockSpec((wq, tq), lambda b, i: (0, b * nq + i)),
                pl.BlockSpec((seq, wq), lambda b, i: (b, 0)),
                pl.BlockSpec((gw, seq), lambda b, i: (0, b))]
    args = [qt, k, vt]
    if has_bias:
        in_specs.append(pl.BlockSpec((seq, bias.shape[1]), lambda b, i: (b, 0)))
        args.append(bias)
    in_specs += [_full(gain.shape), _full(bd.shape)]
    args += [gain, bd]
    return pl.pallas_call(
        functools.partial(_softmax_attn_body, tq=tq, has_bias=has_bias),
        grid=(batch, nq),
        in_specs=in_specs,
        out_specs=pl.BlockSpec((tq, gw), lambda b, i: (b * nq + i, 0)),
        out_shape=jax.ShapeDtypeStruct((t, gw), BF16),
        scratch_shapes=[pltpu.VMEM((N_HEADS, wq, tq), BF16),
                        pltpu.VMEM((N_HEADS, 1, tq), F32),
                        pltpu.VMEM((N_HEADS, 1, tq), F32),
                        pltpu.VMEM((N_HEADS, HEAD_DIM, tq), F32)],
        compiler_params=_params(("arbitrary", "arbitrary")),
        name="fox_attn" if has_bias else "mla_attn",
    )(*args)


def _stickbreak_body(qt_ref, k_ref, vt_ref, gain_ref, bd_ref, tri_ref, o_ref,
                     qm_ref, carry_ref, acc_ref, *, tq):
    qi = pl.program_id(1)
    _mask_queries(qt_ref, qm_ref)
    carry_ref[...] = jnp.zeros_like(carry_ref)
    acc_ref[...] = jnp.zeros_like(acc_ref)
    key = lax.broadcasted_iota(jnp.int32, (tq, tq), 0)
    qry = lax.broadcasted_iota(jnp.int32, (tq, tq), 1)

    def blocks(js, first_is_diagonal):
        offs = [pl.multiple_of(j * tq, tq) for j in js]
        tri = tri_ref[...]
        zs = [[_dot(k_ref[pl.ds(off, tq), :], qm_ref[h]) for h in range(N_HEADS)] for off in offs]
        for n, off in enumerate(offs):
            diagonal = first_is_diagonal and n == 0
            log_betas, afters = [], []
            for h in range(N_HEADS):
                z = zs[n][h]
                log_beta = _log_sigmoid(z)
                log_rest = log_beta - z
                if diagonal:
                    log_rest = jnp.where(key < qry, log_rest, 0.0)
                hi = log_rest.astype(BF16)
                lo = (log_rest - hi.astype(F32)).astype(BF16)
                afters.append(_dot(tri, hi) + _dot(tri, lo) + carry_ref[h])
                log_betas.append(log_beta)
                carry_ref[h] = carry_ref[h] + jnp.sum(log_rest, axis=0, keepdims=True)
            for h in range(N_HEADS):
                w = jnp.exp(log_betas[h] + afters[h])
                if diagonal:
                    w = jnp.where(key < qry, w, 0.0)
                vt = vt_ref[h * HEAD_DIM:(h + 1) * HEAD_DIM, pl.ds(off, tq)]
                acc_ref[h] = acc_ref[h] + _dot(vt, w.astype(BF16))

    @pl.when(qi % 2 == 1)
    def _():
        blocks([qi, qi - 1], True)

    @pl.when(qi % 2 == 0)
    def _():
        blocks([qi], True)

    full = qi - qi % 2

    def body(it, carry):
        blocks([full - 1 - 2 * it, full - 2 - 2 * it], False)
        return carry

    lax.fori_loop(0, full // 2, body, 0)
    y = _finish_heads(acc_ref, lambda h: 1.0, bd_ref[...], gain_ref[...])
    o_ref[...] = y.astype(BF16)


def _stickbreak(qt, k, vt, gain, bd, batch, seq, tq):
    gw, t = qt.shape
    nq = seq // tq
    tri = (jnp.arange(tq)[None, :] > jnp.arange(tq)[:, None]).astype(BF16)
    return pl.pallas_call(
        functools.partial(_stickbreak_body, tq=tq),
        grid=(batch, nq),
        in_specs=[pl.BlockSpec((gw, tq), lambda b, i: (0, b * nq + i)),
                  pl.BlockSpec((seq, gw), lambda b, i: (b, 0)),
                  pl.BlockSpec((gw, seq), lambda b, i: (0, b)),
                  _full(gain.shape), _full(bd.shape), _full(tri.shape)],
        out_specs=pl.BlockSpec((tq, gw), lambda b, i: (b * nq + i, 0)),
        out_shape=jax.ShapeDtypeStruct((t, gw), BF16),
        scratch_shapes=[pltpu.VMEM((N_HEADS, gw, tq), BF16),
                        pltpu.VMEM((N_HEADS, 1, tq), F32),
                        pltpu.VMEM((N_HEADS, HEAD_DIM, tq), F32)],
        compiler_params=_params(("arbitrary", "arbitrary")),
        name="stickbreak_attn",
    )(qt, k, vt, gain, bd, tri)


def _outproj_body(x_ref, ya_ref, yb_ref, yc_ref, yd_ref, wo_ref, gf_ref, wq_ref, sk_ref,
                  x1_ref, h2_ref, sc_ref):
    gw = GROUP_WIDTH
    acc = x_ref[...]
    for n, y_ref in enumerate((ya_ref, yb_ref, yc_ref, yd_ref)):
        acc = acc + _dot(y_ref[...], wo_ref[n * gw:(n + 1) * gw, :])
    x1_ref[...] = acc
    h2 = _rms(acc, gf_ref[...]).astype(BF16)
    h2_ref[...] = h2
    q = _dot(h2, wq_ref[...])
    for hp in range(sc_ref.shape[0]):
        sc_ref[hp] = _dot_nt(sk_ref[hp], q[:, hp * LANES:(hp + 1) * LANES].astype(BF16))


def _outproj(x2, ys, wo, gf, wq, sk, tm):
    t, d = x2.shape
    gw = GROUP_WIDTH
    nk = sk.shape[0]
    tok = lambda w: pl.BlockSpec((tm, w), lambda i: (i, 0))
    return pl.pallas_call(
        _outproj_body,
        grid=(t // tm,),
        in_specs=[tok(d)] + [tok(gw)] * 4 + [_full(wo.shape), _full(gf.shape), _full(wq.shape), _full(sk.shape)],
        out_specs=[tok(d), tok(d), pl.BlockSpec((nk, PEER_KEYS, tm), lambda i: (0, 0, i))],
        out_shape=[jax.ShapeDtypeStruct((t, d), F32), jax.ShapeDtypeStruct((t, d), BF16),
                   jax.ShapeDtypeStruct((nk, PEER_KEYS, t), F32)],
        compiler_params=_params(("arbitrary",)),
        name="outproj_peerq",
    )(x2, *ys, wo, gf, wq, sk)


def _top_k_rows(s, k, big=None):
    rows = lax.broadcasted_iota(jnp.int32, s.shape, 0).astype(F32)
    rank = None if big is None else jnp.full(s.shape, big, F32)
    vals, idxs = [], []
    for r in range(k):
        m = jnp.max(s, axis=0, keepdims=True)
        idx = jnp.min(jnp.where(s == m, rows, float(s.shape[0])), axis=0, keepdims=True)
        hit = rows == idx
        s = jnp.where(hit, NEG_INF, s)
        if rank is not None:
            rank = jnp.where(hit, float(r), rank)
        vals.append(m)
        idxs.append(idx)
    return vals, idxs, rank


def _route_body(sc_ref, rank2_ref, cnt_ref, e1_ref, e2_ref, cand_ref):
    k = PEER_TOPK
    not_selected = float(2 * k)

    def head(h, carry):
        s1 = sc_ref[2 * h]
        s2 = sc_ref[2 * h + 1]
        v1, idx1, _ = _top_k_rows(s1, k)
        v2, _, rank2 = _top_k_rows(s2, k, not_selected)
        cand_ref[...] = jnp.full(cand_ref.shape, NEG_INF, F32)
        for p, (a, b) in enumerate(PEER_PAIRS):
            cand_ref[p:p + 1, :] = v1[a] + v2[b]
        cand = cand_ref[...]
        _, _, crank = _top_k_rows(cand, k, not_selected)
        sel = crank < float(k)
        top = v1[0] + v2[0]
        z = jnp.sum(jnp.where(sel, jnp.exp(cand - top), 0.0), axis=0, keepdims=True)
        prow = lax.broadcasted_iota(jnp.int32, cand.shape, 0)
        keys = lax.broadcasted_iota(jnp.int32, s1.shape, 0).astype(F32)
        cnt = jnp.zeros(s1.shape, F32)
        row0 = 0
        for a in range(k):
            na = k // (a + 1)
            in_a = (prow >= row0) & (prow < row0 + na) & sel
            n_a = jnp.sum(jnp.where(in_a, 1.0, 0.0), axis=0, keepdims=True)
            cnt = cnt + jnp.where(keys == idx1[a], n_a, 0.0)
            row0 += na
        rank2_ref[h] = rank2.astype(BF16)
        cnt_ref[h] = cnt
        e1_ref[h] = jnp.exp(s1 - v1[0]) * (0.5 / z)
        e2_ref[h] = jnp.exp(s2 - v2[0]).astype(BF16)
        return carry

    lax.fori_loop(0, PEER_HEADS, head, 0)


def _route(scores_t, tr):
    nk, nkeys, t = scores_t.shape
    out = lambda dt: jax.ShapeDtypeStruct((PEER_HEADS, nkeys, t), dt)
    spec = pl.BlockSpec((PEER_HEADS, nkeys, tr), lambda i: (0, 0, i))
    return pl.pallas_call(
        _route_body,
        grid=(t // tr,),
        in_specs=[pl.BlockSpec((nk, nkeys, tr), lambda i: (0, 0, i))],
        out_specs=[spec] * 4,
        out_shape=[out(BF16), out(F32), out(F32), out(BF16)],
        scratch_shapes=[pltpu.VMEM((PEER_PAIR_ROWS, tr), F32)],
        compiler_params=_params(("arbitrary",)),
        name="peer_route",
    )(scores_t)


def _peer_body(h_ref, u_ref, vt_ref, rank2_ref, cnt_ref, e1_ref, e2_ref, x1_ref, gfin_ref, o_ref,
               acc_ref, s0_ref, s1_ref, w_ref, *, groups, final_norm):
    n = pl.program_id(1)
    n_slabs = pl.num_programs(1) - 1
    nk = PEER_KEYS
    tm = h_ref.shape[0]
    s_ref = (s0_ref, s1_ref)

    @pl.when(n == 0)
    def _():
        acc_ref[...] = jnp.zeros_like(acc_ref)
        s1_ref[...] = jnp.zeros_like(s1_ref)

    def step(p):
        q = 1 - p
        s_ref[p][...] = _dot_nt(u_ref[...], h_ref[...])
        prev = jnp.clip(n - 1, 0, n_slabs - 1)
        for lanes in (slice(0, tm // 2), slice(tm // 2, tm)):
            for g in range(groups):
                i = prev * groups + g
                s = s_ref[q][g * nk:(g + 1) * nk, lanes]
                act = s * (1.0 + lax.erf(s * (2.0 ** -0.5)))
                gate = None
                for h in range(PEER_HEADS):
                    chosen = rank2_ref[h, :, lanes] < cnt_ref[h, pl.ds(i, 1), lanes].astype(BF16)
                    term = (jnp.where(chosen, e2_ref[h, :, lanes], jnp.zeros((), BF16))
                            * e1_ref[h, pl.ds(i, 1), lanes].astype(BF16))
                    gate = term if gate is None else gate + term
                w_ref[g * nk:(g + 1) * nk, lanes] = gate * act.astype(BF16)
            acc_ref[:, lanes] += _dot(vt_ref[...], w_ref[:, lanes])

    for p in range(2):
        pl.when(n % 2 == p)(functools.partial(step, p))

    @pl.when(n == pl.num_programs(1) - 1)
    def _():
        y = x1_ref[...] + acc_ref[...].T
        if final_norm:
            y = _rms(y, gfin_ref[...])
        o_ref[...] = y


def _peer(h2, u, v, rank2, cnt, e1, e2, x1, gfin, final_norm, tm, groups):
    t, d = h2.shape
    ne = u.shape[0]
    slab = groups * PEER_KEYS
    n_slabs = ne // slab
    vt = v.reshape(n_slabs, slab, d).transpose(0, 2, 1)
    tok = pl.BlockSpec((tm, d), lambda i, s: (i, 0))
    route = pl.BlockSpec((PEER_HEADS, PEER_KEYS, tm), lambda i, s: (0, 0, i))
    return pl.pallas_call(
        functools.partial(_peer_body, groups=groups, final_norm=final_norm),
        grid=(t // tm, n_slabs + 1),
        in_specs=[tok, pl.BlockSpec((slab, d), lambda i, s: (jnp.minimum(s, n_slabs - 1), 0)),
                  pl.BlockSpec((None, d, slab), lambda i, s: (jnp.maximum(s - 1, 0), 0, 0)),
                  route, route, route, route, tok, _full(gfin.shape)],
        out_specs=tok,
        out_shape=jax.ShapeDtypeStruct((t, d), F32),
        scratch_shapes=[pltpu.VMEM((d, tm), F32), pltpu.VMEM((slab, tm), F32), pltpu.VMEM((slab, tm), F32),
                        pltpu.VMEM((slab, tm), BF16)],
        compiler_params=_params(("arbitrary", "arbitrary")),
        name="peer_experts",
    )(h2, u, vt, rank2, cnt, e1, e2, x1, gfin)


def _rope_tables(seq, dim, reps):
    pos = jnp.arange(seq, dtype=F32)
    inv = ROPE_BASE ** (-jnp.arange(0, dim, 2, dtype=F32) / dim)
    ang = pos[:, None] * inv[None, :]
    cos, sin = jnp.cos(ang), jnp.sin(ang)
    return (jnp.tile(jnp.concatenate([cos, cos], axis=1), (1, reps)),
            jnp.tile(jnp.concatenate([-sin, sin], axis=1), (1, reps)), cos.T, sin.T)


def _tile_sizes(seq, tokens):
    pick = lambda n, pref: pref if n % pref == 0 else n
    return dict(tm=pick(seq, 512), chunk=pick(seq, 256), tq=pick(seq, 256),
                tr=pick(tokens, 256), tp=pick(tokens, 512), groups=4)


def kernel(x, norm_mix, w_in, b_forget, norm_cq, norm_ckv, w_uq, w_ukv, head_gain, w_out, norm_ffn, w_query, sub_keys, expert_u, expert_v, norm_final):
    batch, seq, d = x.shape
    depth = w_in.shape[0]
    t = batch * seq
    gw = GROUP_WIDTH
    ts = _tile_sizes(seq, t)
    c64, s64, _, _ = _rope_tables(seq, HEAD_DIM, N_HEADS)
    c32, s32, c32t, s32t = _rope_tables(seq, MLA_ROPE, N_HEADS)
    blk = jnp.arange(gw) // HEAD_DIM
    bd = jnp.where(blk[:, None] == blk[None, :], 1.0 / HEAD_DIM, 0.0).astype(BF16)
    row = lambda v: v.reshape(1, -1).astype(F32)
    pad_lanes = lambda a: jnp.pad(a, ((0, 0), (0, LANES - a.shape[1])))

    x2 = x.reshape(t, d)
    for l in range(depth):
        w = w_in[l]
        cols = lambda k: w[:, k * gw:(k + 1) * gw]
        o_ff = 7 * gw
        o_s = o_ff + N_HEADS
        scols = lambda k: w[:, o_s + k * gw:o_s + (k + 1) * gw]
        o_cq = o_s + 3 * gw
        o_ckv = o_cq + w_uq.shape[1]
        o_kr = o_ckv + w_ukv.shape[1]
        w_main = jnp.concatenate(
            [w[:, :4 * gw], cols(5), scols(1), w[:, o_cq:o_kr], jnp.tile(w[:, o_kr:o_kr + MLA_ROPE], (1, N_HEADS)),
             pad_lanes(w[:, o_ff:o_s])], axis=1).astype(BF16)
        w_t = jnp.concatenate([cols(4), cols(6), scols(0), scols(2)], axis=1).T.astype(BF16)
        bf = pad_lanes(b_forget[l].reshape(1, -1).astype(F32))
        uq = w_uq[l].reshape(-1, N_HEADS, MLA_NOPE + MLA_ROPE)
        wuq_t = jnp.concatenate([uq[:, :, :MLA_NOPE].reshape(-1, gw),
                                 uq[:, :, MLA_NOPE:].reshape(-1, N_HEADS * MLA_ROPE)], axis=1).T.astype(BF16)
        ukv = w_ukv[l].reshape(-1, N_HEADS, MLA_NOPE + HEAD_DIM)
        wukv_n = ukv[:, :, :MLA_NOPE].reshape(-1, gw).astype(BF16)
        wukv_vt = ukv[:, :, MLA_NOPE:].reshape(-1, gw).T.astype(BF16)
        (qa, ka, va, ga, fk, sk, kd, fqt, fvt, sqt, svt, qdt, vdt, fbias) = _inproj(
            x2, row(norm_mix[l]), w_main, w_t, bf, row(norm_cq[l]), row(norm_ckv[l]), wuq_t, wukv_n, wukv_vt,
            c64, s64, c32, s32, c32t, s32t, seq, ts["tm"])
        gain = head_gain[l].astype(F32)
        ya = _retention(qa, ka, va, ga, row(gain[0:gw]), bd, batch, seq, ts["chunk"])
        yb = _softmax_attn(fqt, fk, fvt, fbias, row(gain[gw:2 * gw]), bd, batch, seq, ts["tq"])
        yc = _stickbreak(sqt, sk, svt, row(gain[2 * gw:3 * gw]), bd, batch, seq, ts["tq"])
        yd = _softmax_attn(qdt, kd, vdt, None, row(gain[3 * gw:4 * gw]), bd, batch, seq, ts["tq"])
        keys = sub_keys[l].reshape(2 * PEER_HEADS, PEER_KEYS, -1).astype(BF16)
        x1, h2, scores_t = _outproj(x2, (ya, yb, yc, yd), w_out[l].astype(BF16), row(norm_ffn[l]),
                                    w_query[l].astype(BF16), keys, ts["tm"])
        rank2, cnt, e1, e2 = _route(scores_t, ts["tr"])
        x2 = _peer(h2, expert_u[l].astype(BF16), expert_v[l].astype(BF16), rank2, cnt, e1, e2, x1,
                   row(norm_final), l == depth - 1, ts["tp"], ts["groups"])
    return x2.reshape(batch, seq, d)
```

```python
import functools
import math

import numpy as np
import jax
import jax.numpy as jnp
from jax import lax
from jax.experimental import pallas as pl
from jax.experimental.pallas import tpu as pltpu

F32 = jnp.float32
BF16 = jnp.bfloat16

N_HEADS = 4
HEAD_DIM = 64
GROUP_WIDTH = N_HEADS * HEAD_DIM
MLA_NOPE = 64
MLA_ROPE = 32
ROPE_BASE = 10000.0
PEER_HEADS = 8
PEER_KEYS = 128
PEER_TOPK = 16
EPS = 1e-6
NEG_INF = float("-inf")

LANES = 128
ATTN_BLOCKS_PER_STEP = 4
VMEM_LIMIT = 56 * 1024 * 1024

PEER_PAIRS = tuple((a, b) for a in range(PEER_TOPK) for b in range(PEER_TOPK // (a + 1)))
PEER_PAIR_ROWS = -(-len(PEER_PAIRS) // 8) * 8

_NT = (((1,), (1,)), ((), ()))
_TN = (((0,), (0,)), ((), ()))


def _dot(a, b):
    return jnp.dot(a, b, preferred_element_type=F32)


def _dot_nt(a, b):
    return lax.dot_general(a, b, _NT, preferred_element_type=F32)


def _dot_tn(a, b):
    return lax.dot_general(a, b, _TN, preferred_element_type=F32)


def _split_dot(x, m, terms):
    acc = None
    r = x
    for t in range(terms):
        p = r.astype(BF16)
        d = _dot(p, m)
        acc = d if acc is None else acc + d
        if t + 1 < terms:
            r = r - p.astype(F32)
    return acc


def _log_sigmoid(x):
    return jnp.minimum(x, 0.0) - jnp.log(1.0 + jnp.exp(-jnp.abs(x)))


def _rms(x, g):
    return x * lax.rsqrt(jnp.mean(x * x, axis=-1, keepdims=True) + EPS) * g


def _rope(x, c, s_signed, half):
    outs = []
    for o in range(0, x.shape[1], LANES):
        xs = x[:, o:o + LANES]
        lane = lax.broadcasted_iota(jnp.int32, xs.shape, 1)
        first = (lane & (2 * half - 1)) < half
        rot = jnp.where(first, pltpu.roll(xs, LANES - half, 1), pltpu.roll(xs, half, 1))
        outs.append(xs * c[:, o:o + LANES] + rot * s_signed[:, o:o + LANES])
    return outs[0] if len(outs) == 1 else jnp.concatenate(outs, axis=1)


def _head_of_lane(width):
    lane = lax.broadcasted_iota(jnp.int32, (1, width), 1)
    if width == GROUP_WIDTH:
        return lane >> 6
    return jnp.where(lane < GROUP_WIDTH, lane >> 6, (lane - GROUP_WIDTH) >> 5)


def _params(semantics, flags=None):
    return pltpu.CompilerParams(dimension_semantics=semantics, vmem_limit_bytes=VMEM_LIMIT, flags=flags)


def _full(shape):
    nd = len(shape)
    return pl.BlockSpec(shape, lambda *_: (0,) * nd)


def _inproj_body(x_ref, g_ref, w_ref, wt_ref, bf_ref, ncq_ref, nckv_ref, wuqt_ref, wukv_ref, wukvt_ref,
                 c64_ref, s64_ref, c32_ref, s32_ref, c32t_ref, s32t_ref, tril_ref, rep_ref,
                 qa_ref, ka_ref, va_ref, ga_ref, fk_ref, sk_ref, kd_ref,
                 fqt_ref, fvt_ref, sqt_ref, svt_ref, qdt_ref, vdt_ref, fb_ref, carry_ref, *, tiles_per_seq):
    i = pl.program_id(0)
    hb = _rms(x_ref[...], g_ref[...]).astype(BF16)
    gw = GROUP_WIDTH

    def proj(k, n=gw):
        return _dot(hb, w_ref[:, k * gw:k * gw + n])

    c64, s64 = c64_ref[...], s64_ref[...]
    c32, s32 = c32_ref[...], s32_ref[...]
    scale = HEAD_DIM ** -0.5
    qa_ref[...] = _rope(proj(0), c64, s64, HEAD_DIM // 2).astype(BF16)
    ka_ref[...] = (_rope(proj(1), c64, s64, HEAD_DIM // 2) * scale).astype(BF16)
    va_ref[...] = proj(2).astype(BF16)
    ga_ref[...] = proj(3).astype(BF16)
    fk_ref[...] = proj(4).astype(BF16)
    sk_ref[...] = proj(5).astype(BF16)

    tt = _dot_nt(wt_ref[...], hb)
    fqt_ref[...] = (tt[0:gw] * scale).astype(BF16)
    fvt_ref[...] = tt[gw:2 * gw].astype(BF16)
    sqt_ref[...] = (tt[2 * gw:3 * gw] * scale).astype(BF16)
    svt_ref[...] = tt[3 * gw:4 * gw].astype(BF16)

    mla_scale = (MLA_NOPE + MLA_ROPE) ** -0.5
    cq = _rms(proj(6), ncq_ref[...]).astype(BF16)
    qt = _dot_nt(wuqt_ref[...], cq) * mla_scale
    qdt_ref[0:gw, :] = qt[0:gw].astype(BF16)
    half = MLA_ROPE // 2
    ct, st = c32t_ref[...], s32t_ref[...]
    for h in range(N_HEADS):
        r0 = gw + h * MLA_ROPE
        t1, t2 = qt[r0:r0 + half], qt[r0 + half:r0 + 2 * half]
        qdt_ref[r0:r0 + half, :] = (t1 * ct - t2 * st).astype(BF16)
        qdt_ref[r0 + half:r0 + 2 * half, :] = (t1 * st + t2 * ct).astype(BF16)
    ckv = _rms(proj(7, LANES), nckv_ref[...]).astype(BF16)
    kd_ref[:, 0:gw] = _dot(ckv, wukv_ref[...]).astype(BF16)
    vdt_ref[...] = _dot_nt(wukvt_ref[...], ckv).astype(BF16)
    kr = _dot(hb, w_ref[:, 7 * gw + LANES:7 * gw + 2 * LANES])
    kd_ref[:, gw:gw + LANES] = _rope(kr, c32, s32, MLA_ROPE // 2).astype(BF16)

    lf = _log_sigmoid(proj(8, LANES) + bf_ref[...])
    tril = tril_ref[...]
    cum = None
    r = lf
    for term in range(3):
        part = r.astype(BF16)
        d = _dot(tril, part)
        cum = d if cum is None else cum + d
        if term < 2:
            r = r - part.astype(F32)

    @pl.when(i % tiles_per_seq == 0)
    def _():
        carry_ref[...] = jnp.zeros_like(carry_ref)

    fc = cum + carry_ref[0:1, :]
    tm = fc.shape[0]
    carry_ref[...] = jnp.broadcast_to(fc[tm - 1:tm, :], carry_ref.shape)
    fb_ref[...] = _split_dot(fc, rep_ref[...], 3)


def _inproj(x2, g, w_main, w_t, bf, ncq, nckv, wuq_t, wukv_n, wukv_vt, c64, s64, c32, s32, c32t, s32t, seq, tm):
    t, d = x2.shape
    gw = GROUP_WIDTH
    tiles_per_seq = seq // tm
    tril = (jnp.arange(tm)[:, None] >= jnp.arange(tm)[None, :]).astype(BF16)
    lane_head = jnp.arange(N_HEADS * LANES) // LANES
    rep = (jnp.arange(LANES)[:, None] == lane_head[None, :]).astype(BF16)
    tok = lambda w: pl.BlockSpec((tm, w), lambda i: (i, 0))
    chan = lambda c: pl.BlockSpec((c, tm), lambda i: (0, i))
    pos = lambda w: pl.BlockSpec((tm, w), lambda i: (i % tiles_per_seq, 0))
    pos_t = pl.BlockSpec((MLA_ROPE // 2, tm), lambda i: (0, i % tiles_per_seq))
    tok_widths = [gw] * 6 + [gw + LANES]
    chan_rows = [gw] * 4 + [gw + LANES, gw]
    out_shape = ([jax.ShapeDtypeStruct((t, w), BF16) for w in tok_widths]
                 + [jax.ShapeDtypeStruct((c, t), BF16) for c in chan_rows]
                 + [jax.ShapeDtypeStruct((t, N_HEADS * LANES), F32)])
    out_specs = [tok(w) for w in tok_widths] + [chan(c) for c in chan_rows] + [tok(N_HEADS * LANES)]
    return pl.pallas_call(
        functools.partial(_inproj_body, tiles_per_seq=tiles_per_seq),
        grid=(t // tm,),
        in_specs=[tok(d), _full(g.shape), _full(w_main.shape), _full(w_t.shape), _full(bf.shape),
                  _full(ncq.shape), _full(nckv.shape), _full(wuq_t.shape),
                  _full(wukv_n.shape), _full(wukv_vt.shape),
                  pos(gw), pos(gw), pos(LANES), pos(LANES), pos_t, pos_t, _full(tril.shape), _full(rep.shape)],
        out_specs=out_specs,
        out_shape=out_shape,
        scratch_shapes=[pltpu.VMEM((8, LANES), F32)],
        compiler_params=_params(("arbitrary",)),
        name="inproj",
    )(x2, g, w_main, w_t, bf, ncq, nckv, wuq_t, wukv_n, wukv_vt, c64, s64, c32, s32, c32t, s32t, tril, rep)


def _group_mean(x, bd):
    return _split_dot(x, bd, 2)


def _retention_body(q_ref, k_ref, v_ref, g_ref, gain_ref, bd_ref, o_ref, state_ref, *, chunk):
    ci = pl.program_id(1)

    @pl.when(ci == 0)
    def _():
        state_ref[...] = jnp.zeros_like(state_ref)

    gw = GROUP_WIDTH
    log_gamma = [math.log1p(-(2.0 ** (-5.0 - h))) for h in range(N_HEADS)]
    hl = _head_of_lane(gw)
    lg = jnp.zeros((1, gw), F32)
    for h in range(N_HEADS):
        lg = jnp.where(hl == h, log_gamma[h], lg)

    q, k, v = q_ref[...], k_ref[...], v_ref[...]
    ri = lax.broadcasted_iota(jnp.int32, (chunk, chunk), 0)
    cj = lax.broadcasted_iota(jnp.int32, (chunk, chunk), 1)
    rel = jnp.maximum(ri - cj, 0).astype(F32)
    causal = ri >= cj
    o = jnp.zeros((chunk, gw), F32)
    for h in range(N_HEADS):
        qm = jnp.where(hl == h, q, jnp.zeros_like(q))
        inner = _dot_nt(qm, k) * jnp.where(causal, jnp.exp(log_gamma[h] * rel), 0.0)
        o = o + jnp.where(hl == h, _dot(inner.astype(BF16), v), 0.0)

    pos = lax.broadcasted_iota(jnp.int32, (chunk, 1), 0).astype(F32)
    st = state_ref[...]
    q_dec = (q.astype(F32) * jnp.exp(lg * (pos + 1.0))).astype(BF16)
    o = o + _dot(q_dec, st.astype(BF16))
    k_dec = (k.astype(F32) * jnp.exp(lg * (chunk - 1.0 - pos))).astype(BF16)
    r2 = lax.broadcasted_iota(jnp.int32, (gw, gw), 0) >> 6
    c2 = lax.broadcasted_iota(jnp.int32, (gw, gw), 1) >> 6
    state_ref[...] = jnp.where(r2 == c2, st * jnp.exp(lg * float(chunk)) + _dot_tn(k_dec, v), 0.0)

    bd = bd_ref[...]
    oc = o - _group_mean(o, bd)
    y = oc * lax.rsqrt(_group_mean(oc * oc, bd) + EPS) * gain_ref[...]
    g = g_ref[...].astype(F32)
    o_ref[...] = (y * (g / (1.0 + jnp.exp(-g)))).astype(BF16)


def _retention(qa, ka, va, ga, gain, bd, batch, seq, chunk):
    t, gw = qa.shape
    n = seq // chunk
    blk = pl.BlockSpec((chunk, gw), lambda b, c: (b * n + c, 0))
    return pl.pallas_call(
        functools.partial(_retention_body, chunk=chunk),
        grid=(batch, n),
        in_specs=[blk, blk, blk, blk, _full(gain.shape), _full(bd.shape)],
        out_specs=blk,
        out_shape=jax.ShapeDtypeStruct((t, gw), BF16),
        scratch_shapes=[pltpu.VMEM((gw, gw), F32)],
        compiler_params=_params(("arbitrary", "arbitrary")),
        name="retention",
    )(qa, ka, va, ga, gain, bd)


def _head_of_row(width):
    r = lax.broadcasted_iota(jnp.int32, (width, 1), 0)
    if width == GROUP_WIDTH:
        return r >> 6
    return jnp.where(r < GROUP_WIDTH, r >> 6, (r - GROUP_WIDTH) >> 5)


def _mask_queries(qt_ref, qm_ref):
    qt = qt_ref[...]
    hr = _head_of_row(qt.shape[0])
    for h in range(N_HEADS):
        qm_ref[h] = jnp.where(hr == h, qt, jnp.zeros_like(qt))


def _finish_heads(acc_ref, scale_of_head, bd, gain):
    ot = jnp.concatenate([acc_ref[h] * scale_of_head(h) for h in range(N_HEADS)], axis=0)
    o = ot.T
    return o * lax.rsqrt(_group_mean(o * o, bd) + EPS) * gain


def _softmax_attn_body(*refs, tq, has_bias):
    if has_bias:
        qt_ref, k_ref, vt_ref, bias_ref, gain_ref, bd_ref, o_ref, qm_ref, m_ref, l_ref, acc_ref = refs
    else:
        qt_ref, k_ref, vt_ref, gain_ref, bd_ref, o_ref, qm_ref, m_ref, l_ref, acc_ref = refs
        bias_ref = None
    qi = pl.program_id(1)
    _mask_queries(qt_ref, qm_ref)
    m_ref[...] = jnp.full_like(m_ref, NEG_INF)
    l_ref[...] = jnp.zeros_like(l_ref)
    acc_ref[...] = jnp.zeros_like(acc_ref)
    key = lax.broadcasted_iota(jnp.int32, (tq, tq), 0)
    qry = lax.broadcasted_iota(jnp.int32, (tq, tq), 1)

    def blocks(js, last_is_diagonal):
        offs = [pl.multiple_of(j * tq, tq) for j in js]
        scores = [[_dot(k_ref[pl.ds(off, tq), :], qm_ref[h]) for h in range(N_HEADS)] for off in offs]
        for n, off in enumerate(offs):
            diagonal = last_is_diagonal and n == len(offs) - 1
            for h in range(N_HEADS):
                s = scores[n][h]
                if has_bias:
                    kbias = bias_ref[pl.ds(off, tq), h * LANES:(h + 1) * LANES]
                    s = s - jnp.concatenate([kbias] * (tq // LANES), axis=1)
                if diagonal:
                    s = jnp.where(key <= qry, s, NEG_INF)
                m_prev = m_ref[h]
                m_new = jnp.maximum(m_prev, jnp.max(s, axis=0, keepdims=True))
                alpha = jnp.exp(m_prev - m_new)
                p = jnp.exp(s - m_new)
                l_ref[h] = alpha * l_ref[h] + jnp.sum(p, axis=0, keepdims=True)
                vt = vt_ref[h * HEAD_DIM:(h + 1) * HEAD_DIM, pl.ds(off, tq)]
                acc_ref[h] = alpha * acc_ref[h] + _dot(vt, p.astype(BF16))
                m_ref[h] = m_new

    nb = ATTN_BLOCKS_PER_STEP

    def body(j, carry):
        blocks([nb * j + n for n in range(nb)], False)
        return carry

    lax.fori_loop(0, qi // nb, body, 0)
    for r in range(nb):
        pl.when(qi % nb == r)(functools.partial(blocks, [qi - r + n for n in range(r + 1)], True))

    y = _finish_heads(acc_ref, lambda h: 1.0 / l_ref[h], bd_ref[...], gain_ref[...])
    o_ref[...] = y.astype(BF16)


def _softmax_attn(qt, k, vt, bias, gain, bd, batch, seq, tq):
    wq, t = qt.shape
    gw = GROUP_WIDTH
    nq = seq // tq
    has_bias = bias is not None
    in_specs = [pl.BlockSpec((wq, tq), lambda b, i: (0, b * nq + i)),
                pl.BlockSpec((seq, wq), lambda b, i: (b, 0)),
                pl.BlockSpec((gw, seq), lambda b, i: (0, b))]
    args = [qt, k, vt]
    if has_bias:
        in_specs.append(pl.BlockSpec((seq, bias.shape[1]), lambda b, i: (b, 0)))
        args.append(bias)
    in_specs += [_full(gain.shape), _full(bd.shape)]
    args += [gain, bd]
    return pl.pallas_call(
        functools.partial(_softmax_attn_body, tq=tq, has_bias=has_bias),
        grid=(batch, nq),
        in_specs=in_specs,
        out_specs=pl.BlockSpec((tq, gw), lambda b, i: (b * nq + i, 0)),
        out_shape=jax.ShapeDtypeStruct((t, gw), BF16),
        scratch_shapes=[pltpu.VMEM((N_HEADS, wq, tq), BF16),
                        pltpu.VMEM((N_HEADS, 1, tq), F32),
                        pltpu.VMEM((N_HEADS, 1, tq), F32),
                        pltpu.VMEM((N_HEADS, HEAD_DIM, tq), F32)],
        compiler_params=_params(("arbitrary", "arbitrary")),
        name="fox_attn" if has_bias else "mla_attn",
    )(*args)


def _stickbreak_body(qt_ref, k_ref, vt_ref, gain_ref, bd_ref, tri_ref, o_ref,
                     qm_ref, carry_ref, acc_ref, *, tq):
    qi = pl.program_id(1)
    _mask_queries(qt_ref, qm_ref)
    carry_ref[...] = jnp.zeros_like(carry_ref)
    acc_ref[...] = jnp.zeros_like(acc_ref)
    key = lax.broadcasted_iota(jnp.int32, (tq, tq), 0)
    qry = lax.broadcasted_iota(jnp.int32, (tq, tq), 1)

    def blocks(js, first_is_diagonal):
        offs = [pl.multiple_of(j * tq, tq) for j in js]
        tri = tri_ref[...]
        zs = [[_dot(k_ref[pl.ds(off, tq), :], qm_ref[h]) for h in range(N_HEADS)] for off in offs]
        for n, off in enumerate(offs):
            diagonal = first_is_diagonal and n == 0
            log_betas, afters = [], []
            for h in range(N_HEADS):
                z = zs[n][h]
                log_beta = _log_sigmoid(z)
                log_rest = log_beta - z
                if diagonal:
                    log_rest = jnp.where(key < qry, log_rest, 0.0)
                hi = log_rest.astype(BF16)
                lo = (log_rest - hi.astype(F32)).astype(BF16)
                afters.append(_dot(tri, hi) + _dot(tri, lo) + carry_ref[h])
                log_betas.append(log_beta)
                carry_ref[h] = carry_ref[h] + jnp.sum(log_rest, axis=0, keepdims=True)
            for h in range(N_HEADS):
                w = jnp.exp(log_betas[h] + afters[h])
                if diagonal:
                    w = jnp.where(key < qry, w, 0.0)
                vt = vt_ref[h * HEAD_DIM:(h + 1) * HEAD_DIM, pl.ds(off, tq)]
                acc_ref[h] = acc_ref[h] + _dot(vt, w.astype(BF16))

    nb = ATTN_BLOCKS_PER_STEP
    for r in range(nb):
        pl.when(qi % nb == r)(functools.partial(blocks, [qi - n for n in range(r + 1)], True))
    full = qi - qi % nb

    def body(it, carry):
        blocks([full - 1 - nb * it - n for n in range(nb)], False)
        return carry

    lax.fori_loop(0, full // nb, body, 0)
    y = _finish_heads(acc_ref, lambda h: 1.0, bd_ref[...], gain_ref[...])
    o_ref[...] = y.astype(BF16)


def _stickbreak(qt, k, vt, gain, bd, batch, seq, tq):
    gw, t = qt.shape
    nq = seq // tq
    tri = (jnp.arange(tq)[None, :] > jnp.arange(tq)[:, None]).astype(BF16)
    return pl.pallas_call(
        functools.partial(_stickbreak_body, tq=tq),
        grid=(batch, nq),
        in_specs=[pl.BlockSpec((gw, tq), lambda b, i: (0, b * nq + i)),
                  pl.BlockSpec((seq, gw), lambda b, i: (b, 0)),
                  pl.BlockSpec((gw, seq), lambda b, i: (0, b)),
                  _full(gain.shape), _full(bd.shape), _full(tri.shape)],
        out_specs=pl.BlockSpec((tq, gw), lambda b, i: (b * nq + i, 0)),
        out_shape=jax.ShapeDtypeStruct((t, gw), BF16),
        scratch_shapes=[pltpu.VMEM((N_HEADS, gw, tq), BF16),
                        pltpu.VMEM((N_HEADS, 1, tq), F32),
                        pltpu.VMEM((N_HEADS, HEAD_DIM, tq), F32)],
        compiler_params=_params(("arbitrary", "arbitrary")),
        name="stickbreak_attn",
    )(qt, k, vt, gain, bd, tri)


def _outproj_body(x_ref, ya_ref, yb_ref, yc_ref, yd_ref, wo_ref, gf_ref, wq_ref, sk_ref,
                  x1_ref, h2_ref, sc_ref):
    gw = GROUP_WIDTH
    acc = x_ref[...]
    for n, y_ref in enumerate((ya_ref, yb_ref, yc_ref, yd_ref)):
        acc = acc + _dot(y_ref[...], wo_ref[n * gw:(n + 1) * gw, :])
    x1_ref[...] = acc
    h2 = _rms(acc, gf_ref[...]).astype(BF16)
    h2_ref[...] = h2
    q = _dot(h2, wq_ref[...])
    for hp in range(sc_ref.shape[0]):
        sc_ref[hp] = _dot_nt(sk_ref[hp], q[:, hp * LANES:(hp + 1) * LANES].astype(BF16))


def _outproj(x2, ys, wo, gf, wq, sk, tm):
    t, d = x2.shape
    gw = GROUP_WIDTH
    nk = sk.shape[0]
    tok = lambda w: pl.BlockSpec((tm, w), lambda i: (i, 0))
    return pl.pallas_call(
        _outproj_body,
        grid=(t // tm,),
        in_specs=[tok(d)] + [tok(gw)] * 4 + [_full(wo.shape), _full(gf.shape), _full(wq.shape), _full(sk.shape)],
        out_specs=[tok(d), tok(d), pl.BlockSpec((nk, PEER_KEYS, tm), lambda i: (0, 0, i))],
        out_shape=[jax.ShapeDtypeStruct((t, d), F32), jax.ShapeDtypeStruct((t, d), BF16),
                   jax.ShapeDtypeStruct((nk, PEER_KEYS, t), F32)],
        compiler_params=_params(("arbitrary",)),
        name="outproj_peerq",
    )(x2, *ys, wo, gf, wq, sk)


def _top_k_rows(s, k, big=None):
    rows = lax.broadcasted_iota(jnp.int32, s.shape, 0).astype(F32)
    rank = None if big is None else jnp.full(s.shape, big, F32)
    vals, idxs = [], []
    for r in range(k):
        m = jnp.max(s, axis=0, keepdims=True)
        idx = jnp.min(jnp.where(s == m, rows, float(s.shape[0])), axis=0, keepdims=True)
        hit = rows == idx
        s = jnp.where(hit, NEG_INF, s)
        if rank is not None:
            rank = jnp.where(hit, float(r), rank)
        vals.append(m)
        idxs.append(idx)
    return vals, idxs, rank


def _route_body(sc_ref, rank2_ref, cnt_ref, e1_ref, e2_ref, cand_ref):
    k = PEER_TOPK
    not_selected = float(2 * k)

    def head(h, carry):
        s1 = sc_ref[2 * h]
        s2 = sc_ref[2 * h + 1]
        v1, idx1, _ = _top_k_rows(s1, k)
        v2, _, rank2 = _top_k_rows(s2, k, not_selected)
        cand_ref[...] = jnp.full(cand_ref.shape, NEG_INF, F32)
        for p, (a, b) in enumerate(PEER_PAIRS):
            cand_ref[p:p + 1, :] = v1[a] + v2[b]
        cand = cand_ref[...]
        _, _, crank = _top_k_rows(cand, k, not_selected)
        sel = crank < float(k)
        top = v1[0] + v2[0]
        z = jnp.sum(jnp.where(sel, jnp.exp(cand - top), 0.0), axis=0, keepdims=True)
        prow = lax.broadcasted_iota(jnp.int32, cand.shape, 0)
        keys = lax.broadcasted_iota(jnp.int32, s1.shape, 0).astype(F32)
        cnt = jnp.zeros(s1.shape, F32)
        row0 = 0
        for a in range(k):
            na = k // (a + 1)
            in_a = (prow >= row0) & (prow < row0 + na) & sel
            n_a = jnp.sum(jnp.where(in_a, 1.0, 0.0), axis=0, keepdims=True)
            cnt = jnp.where(keys == idx1[a], n_a, cnt)
            row0 += na
        rank2_ref[h] = rank2.astype(BF16)
        cnt_ref[h] = cnt
        e1_ref[h] = jnp.exp(s1 - v1[0]) * (0.5 / z)
        e2_ref[h] = jnp.exp(s2 - v2[0]).astype(BF16)
        return carry

    lax.fori_loop(0, PEER_HEADS, head, 0)


def _route(scores_t, tr):
    nk, nkeys, t = scores_t.shape
    out = lambda dt: jax.ShapeDtypeStruct((PEER_HEADS, nkeys, t), dt)
    spec = pl.BlockSpec((PEER_HEADS, nkeys, tr), lambda i: (0, 0, i))
    return pl.pallas_call(
        _route_body,
        grid=(t // tr,),
        in_specs=[pl.BlockSpec((nk, nkeys, tr), lambda i: (0, 0, i))],
        out_specs=[spec] * 4,
        out_shape=[out(BF16), out(F32), out(F32), out(BF16)],
        scratch_shapes=[pltpu.VMEM((PEER_PAIR_ROWS, tr), F32)],
        compiler_params=_params(("arbitrary",)),
        name="peer_route",
    )(scores_t)


def _peer_body(h_ref, u_ref, vt_ref, rank2_ref, cnt_ref, e1_ref, e2_ref, x1_ref, gfin_ref, o_ref,
               acc_ref, s0_ref, s1_ref, w_ref, *, groups, final_norm):
    n = pl.program_id(1)
    n_slabs = pl.num_programs(1) - 1
    nk = PEER_KEYS
    tm = h_ref.shape[0]
    s_ref = (s0_ref, s1_ref)

    @pl.when(n == 0)
    def _():
        acc_ref[...] = jnp.zeros_like(acc_ref)
        s1_ref[...] = jnp.zeros_like(s1_ref)

    def step(p):
        q = 1 - p
        s_ref[p][...] = _dot_nt(u_ref[...], h_ref[...])
        prev = jnp.clip(n - 1, 0, n_slabs - 1)
        for lanes in (slice(0, tm // 2), slice(tm // 2, tm)):
            for g in range(groups):
                i = prev * groups + g
                s = s_ref[q][g * nk:(g + 1) * nk, lanes]
                sb = s.astype(BF16)
                act = sb * (1.0 + lax.erf(sb * (2.0 ** -0.5)))
                gate = None
                for h in range(PEER_HEADS):
                    chosen = rank2_ref[h, :, lanes] < cnt_ref[h, pl.ds(i, 1), lanes].astype(BF16)
                    term = (jnp.where(chosen, e2_ref[h, :, lanes], jnp.zeros((), BF16))
                            * e1_ref[h, pl.ds(i, 1), lanes].astype(BF16))
                    gate = term if gate is None else gate + term
                w_ref[g * nk:(g + 1) * nk, lanes] = gate * act.astype(BF16)
            acc_ref[:, lanes] += _dot(vt_ref[...], w_ref[:, lanes])

    for p in range(2):
        pl.when(n % 2 == p)(functools.partial(step, p))

    @pl.when(n == pl.num_programs(1) - 1)
    def _():
        y = x1_ref[...] + acc_ref[...].T
        if final_norm:
            y = _rms(y, gfin_ref[...])
        o_ref[...] = y


def _peer(h2, u, v, rank2, cnt, e1, e2, x1, gfin, final_norm, tm, groups):
    t, d = h2.shape
    ne = u.shape[0]
    slab = groups * PEER_KEYS
    n_slabs = ne // slab
    vt = v.reshape(n_slabs, slab, d).transpose(0, 2, 1)
    tok = pl.BlockSpec((tm, d), lambda i, s: (i, 0))
    route = pl.BlockSpec((PEER_HEADS, PEER_KEYS, tm), lambda i, s: (0, 0, i))
    return pl.pallas_call(
        functools.partial(_peer_body, groups=groups, final_norm=final_norm),
        grid=(t // tm, n_slabs + 1),
        in_specs=[tok, pl.BlockSpec((slab, d), lambda i, s: (jnp.minimum(s, n_slabs - 1), 0)),
                  pl.BlockSpec((None, d, slab), lambda i, s: (jnp.maximum(s - 1, 0), 0, 0)),
                  route, route, route, route, tok, _full(gfin.shape)],
        out_specs=tok,
        out_shape=jax.ShapeDtypeStruct((t, d), F32),
        scratch_shapes=[pltpu.VMEM((d, tm), F32), pltpu.VMEM((slab, tm), F32), pltpu.VMEM((slab, tm), F32),
                        pltpu.VMEM((slab, tm), BF16)],
        compiler_params=_params(("arbitrary", "arbitrary")),
        name="peer_experts",
    )(h2, u, vt, rank2, cnt, e1, e2, x1, gfin)


def _rope_tables(seq, dim, reps):
    pos = jnp.arange(seq, dtype=F32)
    inv = ROPE_BASE ** (-jnp.arange(0, dim, 2, dtype=F32) / dim)
    ang = pos[:, None] * inv[None, :]
    cos, sin = jnp.cos(ang), jnp.sin(ang)
    return (jnp.tile(jnp.concatenate([cos, cos], axis=1), (1, reps)),
            jnp.tile(jnp.concatenate([-sin, sin], axis=1), (1, reps)), cos.T, sin.T)


def _tile_sizes(seq, tokens):
    pick = lambda n, pref: pref if n % pref == 0 else n
    return dict(tm=pick(seq, 512), chunk=pick(seq, 256), tq=pick(seq, 256),
                tr=pick(tokens, 256), tp=pick(tokens, 512), groups=4)


def kernel(x, norm_mix, w_in, b_forget, norm_cq, norm_ckv, w_uq, w_ukv, head_gain, w_out, norm_ffn, w_query, sub_keys, expert_u, expert_v, norm_final):
    batch, seq, d = x.shape
    depth = w_in.shape[0]
    t = batch * seq
    gw = GROUP_WIDTH
    ts = _tile_sizes(seq, t)
    c64, s64, _, _ = _rope_tables(seq, HEAD_DIM, N_HEADS)
    c32, s32, c32t, s32t = _rope_tables(seq, MLA_ROPE, N_HEADS)
    blk = jnp.arange(gw) // HEAD_DIM
    bd = jnp.where(blk[:, None] == blk[None, :], 1.0 / HEAD_DIM, 0.0).astype(BF16)
    row = lambda v: v.reshape(1, -1).astype(F32)
    pad_lanes = lambda a: jnp.pad(a, ((0, 0), (0, LANES - a.shape[1])))

    x2 = x.reshape(t, d)
    for l in range(depth):
        w = w_in[l]
        cols = lambda k: w[:, k * gw:(k + 1) * gw]
        o_ff = 7 * gw
        o_s = o_ff + N_HEADS
        scols = lambda k: w[:, o_s + k * gw:o_s + (k + 1) * gw]
        o_cq = o_s + 3 * gw
        o_ckv = o_cq + w_uq.shape[1]
        o_kr = o_ckv + w_ukv.shape[1]
        w_main = jnp.concatenate(
            [w[:, :4 * gw], cols(5), scols(1), w[:, o_cq:o_kr], jnp.tile(w[:, o_kr:o_kr + MLA_ROPE], (1, N_HEADS)),
             pad_lanes(w[:, o_ff:o_s])], axis=1).astype(BF16)
        w_t = jnp.concatenate([cols(4), cols(6), scols(0), scols(2)], axis=1).T.astype(BF16)
        bf = pad_lanes(b_forget[l].reshape(1, -1).astype(F32))
        uq = w_uq[l].reshape(-1, N_HEADS, MLA_NOPE + MLA_ROPE)
        wuq_t = jnp.concatenate([uq[:, :, :MLA_NOPE].reshape(-1, gw),
                                 uq[:, :, MLA_NOPE:].reshape(-1, N_HEADS * MLA_ROPE)], axis=1).T.astype(BF16)
        ukv = w_ukv[l].reshape(-1, N_HEADS, MLA_NOPE + HEAD_DIM)
        wukv_n = ukv[:, :, :MLA_NOPE].reshape(-1, gw).astype(BF16)
        wukv_vt = ukv[:, :, MLA_NOPE:].reshape(-1, gw).T.astype(BF16)
        (qa, ka, va, ga, fk, sk, kd, fqt, fvt, sqt, svt, qdt, vdt, fbias) = _inproj(
            x2, row(norm_mix[l]), w_main, w_t, bf, row(norm_cq[l]), row(norm_ckv[l]), wuq_t, wukv_n, wukv_vt,
            c64, s64, c32, s32, c32t, s32t, seq, ts["tm"])
        gain = head_gain[l].astype(F32)
        ya = _retention(qa, ka, va, ga, row(gain[0:gw]), bd, batch, seq, ts["chunk"])
        yb = _softmax_attn(fqt, fk, fvt, fbias, row(gain[gw:2 * gw]), bd, batch, seq, ts["tq"])
        yc = _stickbreak(sqt, sk, svt, row(gain[2 * gw:3 * gw]), bd, batch, seq, ts["tq"])
        yd = _softmax_attn(qdt, kd, vdt, None, row(gain[3 * gw:4 * gw]), bd, batch, seq, ts["tq"])
        keys = sub_keys[l].reshape(2 * PEER_HEADS, PEER_KEYS, -1).astype(BF16)
        x1, h2, scores_t = _outproj(x2, (ya, yb, yc, yd), w_out[l].astype(BF16), row(norm_ffn[l]),
                                    w_query[l].astype(BF16), keys, ts["tm"])
        rank2, cnt, e1, e2 = _route(scores_t, ts["tr"])
        x2 = _peer(h2, expert_u[l].astype(BF16), expert_v[l].astype(BF16), rank2, cnt, e1, e2, x1,
                   row(norm_final), l == depth - 1, ts["tp"], ts["groups"])
    return x2.reshape(batch, seq, d)
```
